```python
import math
import jax
import jax.numpy as jnp
from jax import lax
import numpy as np

D_MODEL = 1024
BATCH = 8
SEQ = 4096
DEPTH = 4

GRID_W = 64
CTX_LEN = 256
N_MIXERS = 3
Q_BLOCK = 128
ROPE_THETA = 10000.0
NORM_EPS = 1e-6
N_MOD = 9
D_FF = 2816

DA_HEADS = 8
DA_HEAD_DIM = D_MODEL // (2 * DA_HEADS)
GQ_HEADS = 8
GQ_KV_HEADS = 2
GQ_HEAD_DIM = D_MODEL // GQ_HEADS
GQ_GROUP = GQ_HEADS // GQ_KV_HEADS
GQ_QKV = (GQ_HEADS + 2 * GQ_KV_HEADS) * GQ_HEAD_DIM
RW_HEAD = 64
RW_HEADS = D_MODEL // RW_HEAD
RW_DECAY_LORA = 64
RW_AAA_LORA = 64
RW_GATE_LORA = 128
RW_GN_EPS = 64e-5

kernel_name = "hybrid_dit_diffattn_gqa_rwkv7_macaron"


def rms_norm(x, gain=None):
    xf = x.astype(jnp.float32)
    y = xf * lax.rsqrt(jnp.mean(xf * xf, axis=-1, keepdims=True) + NORM_EPS)
    if gain is not None:
        y = y * gain.astype(jnp.float32)
    return y.astype(x.dtype)


def modulate(h, shift, scale):
    return rms_norm(h) * (1 + scale) + shift


def swiglu(x, wg, wu, wd):
    return (jax.nn.silu(x @ wg) * (x @ wu)) @ wd


def axial_rope(n_tokens, head_dim, dtype):
    rows = n_tokens // GRID_W
    row = jnp.repeat(jnp.arange(rows, dtype=jnp.int32), GRID_W).astype(jnp.float32)
    col = jnp.tile(jnp.arange(GRID_W, dtype=jnp.int32), rows).astype(jnp.float32)
    n_freq = head_dim // 4
    inv = jnp.power(ROPE_THETA, -jnp.arange(n_freq, dtype=jnp.float32) / n_freq)
    ang = jnp.concatenate([row[:, None] * inv, col[:, None] * inv], axis=-1)
    return jnp.cos(ang).astype(dtype), jnp.sin(ang).astype(dtype)


def apply_rope(x, cos, sin):
    shape = (cos.shape[0],) + (1,) * (x.ndim - 3) + (cos.shape[-1],)
    cos = cos.reshape(shape)
    sin = sin.reshape(shape)
    x1, x2 = jnp.split(x, 2, axis=-1)
    return jnp.concatenate([x1 * cos - x2 * sin, x2 * cos + x1 * sin], axis=-1)


def sweep_query_blocks(fn, q):
    b, s = q.shape[:2]
    nb = s // Q_BLOCK
    qb = jnp.moveaxis(q.reshape((b, nb, Q_BLOCK) + q.shape[2:]), 1, 0)
    out = jnp.moveaxis(lax.map(fn, qb), 0, 1)
    return out.reshape((b, s) + out.shape[3:])


def diff_attn_core(q, k, v, lam):
    s = jnp.einsum('bqhcd,bkhcd->bhcqk', q, k, preferred_element_type=jnp.float32) * (DA_HEAD_DIM ** -0.5)
    p = jax.nn.softmax(s, axis=-1)
    a = p[:, :, 0] - lam * p[:, :, 1]
    return jnp.einsum('bhqk,bkhe->bqhe', a.astype(v.dtype), v)


def differential_attention(u, uc, wqkv, wo, lam_vecs, subln_g, lam_init, need_ctx):
    b, s, _ = u.shape

    def project(z):
        t = z.shape[1]
        q, k, v = jnp.split(z @ wqkv, 3, axis=-1)
        return (q.reshape(b, t, DA_HEADS, 2, DA_HEAD_DIM),
                k.reshape(b, t, DA_HEADS, 2, DA_HEAD_DIM),
                v.reshape(b, t, DA_HEADS, 2 * DA_HEAD_DIM))

    q, k, v = project(u)
    qc, kc, vc = project(uc)
    cos, sin = axial_rope(s, DA_HEAD_DIM, u.dtype)
    q = apply_rope(q, cos, sin)
    k = apply_rope(k, cos, sin)
    lv = lam_vecs.astype(jnp.float32)
    lam = jnp.exp(jnp.sum(lv[0] * lv[1])) - jnp.exp(jnp.sum(lv[2] * lv[3])) + lam_init
    k_all = jnp.concatenate([kc, k], axis=1)
    v_all = jnp.concatenate([vc, v], axis=1)

    def finish(o):
        o = rms_norm(o, subln_g) * (1.0 - lam_init)
        return o.reshape(o.shape[0], o.shape[1], D_MODEL) @ wo

    y = finish(sweep_query_blocks(lambda qb: diff_attn_core(qb, k_all, v_all, lam), q))
    yc = finish(diff_attn_core(qc, kc, vc, lam)) if need_ctx else None
    return y, yc


def gqa_core(q, k, v):
    s = jnp.einsum('bqgmd,bkgd->bgmqk', q, k, preferred_element_type=jnp.float32) * (GQ_HEAD_DIM ** -0.5)
    p = jax.nn.softmax(s, axis=-1)
    return jnp.einsum('bgmqk,bkgd->bqgmd', p.astype(v.dtype), v)


def grouped_query_attention(u, uc, wqkv, wo, qk_g, need_ctx):
    b, s, _ = u.shape
    nq = GQ_HEADS * GQ_HEAD_DIM
    nk = GQ_KV_HEADS * GQ_HEAD_DIM

    def project(z):
        t = z.shape[1]
        qkv = z @ wqkv
        q = qkv[..., :nq].reshape(b, t, GQ_KV_HEADS, GQ_GROUP, GQ_HEAD_DIM)
        k = qkv[..., nq:nq + nk].reshape(b, t, GQ_KV_HEADS, GQ_HEAD_DIM)
        v = qkv[..., nq + nk:].reshape(b, t, GQ_KV_HEADS, GQ_HEAD_DIM)
        return rms_norm(q, qk_g[0]), rms_norm(k, qk_g[1]), v

    q, k, v = project(u)
    qc, kc, vc = project(uc)
    cos, sin = axial_rope(s, GQ_HEAD_DIM, u.dtype)
    q = apply_rope(q, cos, sin)
    k = apply_rope(k, cos, sin)
    k_all = jnp.concatenate([kc, k], axis=1)
    v_all = jnp.concatenate([vc, v], axis=1)

    def finish(o):
        return o.reshape(o.shape[0], o.shape[1], D_MODEL) @ wo

    y = finish(sweep_query_blocks(lambda qb: gqa_core(qb, k_all, v_all), q))
    yc = finish(gqa_core(qc, kc, vc)) if need_ctx else None
    return y, yc


def rwkv_token_inputs(z, mu, wr, wk, wv, w0, w1, w2, a0, a1, a2, g1, g2, k_k, k_a, r_k):
    b, t, _ = z.shape
    zero = jnp.zeros_like(z[:, :1])
    d_prev = jnp.concatenate([zero, z[:, :-1]], axis=1) - z
    d_next = jnp.concatenate([z[:, 1:], zero], axis=1) - z

    def mix(i):
        return z + d_prev * mu[0, i] + d_next * mu[1, i]

    def heads(y):
        return y.reshape(b, t, RW_HEADS, RW_HEAD)

    xw, xa, xg = mix(1), mix(4), mix(5)
    r = heads(mix(0) @ wr)
    k = mix(2) @ wk
    v = heads(mix(3) @ wv)
    dirs = []
    for d in range(2):
        w_log = -jax.nn.softplus(-(w0[d] + jnp.tanh(xw @ w1[d]) @ w2[d])) - 0.5
        decay = jnp.exp(-jnp.exp(w_log.astype(jnp.float32)))
        a = jax.nn.sigmoid(a0[d] + (xa @ a1[d]) @ a2[d])
        g = jax.nn.sigmoid(xg @ g1[d]) @ g2[d]
        kk = heads(k * k_k[d]).astype(jnp.float32)
        kk = kk / jnp.maximum(jnp.linalg.norm(kk, axis=-1, keepdims=True), 1e-12)
        kd = heads(k * (1 + (a - 1) * k_a[d]))
        bonus = jnp.sum((r * kd * r_k[d]).astype(jnp.float32), axis=-1, keepdims=True) * v.astype(jnp.float32)
        dirs.append((heads(decay), kd, -kk, kk * heads(a).astype(jnp.float32), g, bonus))
    return r, v, dirs


def wkv7_scan(state, r, decay, k, v, a, b, reverse):
    xs = tuple(jnp.moveaxis(z.astype(jnp.float32), 1, 0) for z in (r, decay, k, v, a, b))

    def step(S, inp):
        r_t, w_t, k_t, v_t, a_t, b_t = inp
        sa = jnp.einsum('bhvk,bhk->bhv', S, a_t)
        S = S * w_t[:, :, None, :] + sa[..., None] * b_t[:, :, None, :] + v_t[..., None] * k_t[:, :, None, :]
        return S, jnp.einsum('bhvk,bhk->bhv', S, r_t)

    state, ys = lax.scan(step, state, xs, reverse=reverse)
    return state, jnp.moveaxis(ys, 0, 1)


def rwkv_readout(y, bonus, g, ln_g, ln_b, dtype):
    b, t = y.shape[:2]
    mean = jnp.mean(y, axis=-1, keepdims=True)
    var = jnp.mean(jnp.square(y - mean), axis=-1, keepdims=True)
    yn = ((y - mean) * lax.rsqrt(var + RW_GN_EPS)).reshape(b, t, D_MODEL)
    out = yn * ln_g.astype(jnp.float32) + ln_b.astype(jnp.float32) + bonus.reshape(b, t, D_MODEL)
    return out.astype(dtype) * g


def rwkv7_bidirectional(u, uc, mu, wr, wk, wv, wo, w0, w1, w2, a0, a1, a2, g1, g2, k_k, k_a, r_k,
                        ln_g, ln_b, need_ctx):
    params = (mu, wr, wk, wv, w0, w1, w2, a0, a1, a2, g1, g2, k_k, k_a, r_k)
    r, v, dirs = rwkv_token_inputs(u, *params)
    rc, vc, dirs_c = rwkv_token_inputs(uc, *params)
    state0 = jnp.zeros((u.shape[0], RW_HEADS, RW_HEAD, RW_HEAD), jnp.float32)
    outs, outs_c = [], []
    for d in range(2):
        reverse = d == 1
        dc, kdc, ac, bc, gc, bonc = dirs_c[d]
        dl, kdl, al, bl, gl, bonl = dirs[d]
        state_c, wkv_c = wkv7_scan(state0, rc, dc, kdc, vc, ac, bc, reverse)
        _, wkv_l = wkv7_scan(state_c, r, dl, kdl, v, al, bl, reverse)
        outs.append(rwkv_readout(wkv_l, bonl, gl, ln_g[d], ln_b[d], u.dtype))
        if need_ctx:
            outs_c.append(rwkv_readout(wkv_c, bonc, gc, ln_g[d], ln_b[d], u.dtype))
    y = (outs[0] + outs[1]) @ wo
    yc = (outs_c[0] + outs_c[1]) @ wo if need_ctx else None
    return y, yc


def setup_inputs(seed: int = 0) -> dict:
    key = jax.random.key(seed)
    ks = iter(jax.random.split(key, 48))
    D = D_MODEL
    n_a = (DEPTH + 2) // N_MIXERS
    n_b = (DEPTH + 1) // N_MIXERS
    n_c = DEPTH // N_MIXERS

    def nrm(shape, scale):
        return jax.random.normal(next(ks), shape, jnp.float32) * scale

    def uni(shape, lo, hi):
        return jax.random.uniform(next(ks), shape, jnp.float32, lo, hi)

    return {
        "x": nrm((BATCH, SEQ, D), 1.0),
        "c": nrm((BATCH, D), 1.0),
        "ctx": nrm((BATCH, CTX_LEN, D), 1.0),
        "c_ctx": nrm((D,), 1.0),
        "ada_w": nrm((DEPTH, D, N_MOD * D), 0.5 * D ** -0.5),
        "ada_b": nrm((DEPTH, N_MOD * D), 0.02),
        "ffn_wg": nrm((DEPTH, 2, D, D_FF), D ** -0.5),
        "ffn_wu": nrm((DEPTH, 2, D, D_FF), D ** -0.5),
        "ffn_wd": nrm((DEPTH, 2, D_FF, D), D_FF ** -0.5),
        "final_g": 1.0 + nrm((D,), 0.02),
        "da_wqkv": nrm((n_a, D, 3 * D), D ** -0.5),
        "da_wo": nrm((n_a, D, D), D ** -0.5),
        "da_lam": nrm((n_a, 4, DA_HEAD_DIM), 0.1),
        "da_subln": 1.0 + nrm((n_a, 2 * DA_HEAD_DIM), 0.02),
        "gq_wqkv": nrm((n_b, D, GQ_QKV), D ** -0.5),
        "gq_wo": nrm((n_b, GQ_HEADS * GQ_HEAD_DIM, D), D ** -0.5),
        "gq_qk_g": 1.0 + nrm((n_b, 2, GQ_HEAD_DIM), 0.02),
        "rw_mu": uni((n_c, 2, 6, D), 0.0, 0.5),
        "rw_wr": nrm((n_c, D, D), D ** -0.5),
        "rw_wk": nrm((n_c, D, D), D ** -0.5),
        "rw_wv": nrm((n_c, D, D), D ** -0.5),
        "rw_wo": nrm((n_c, D, D), D ** -0.5),
        "rw_w0": uni((n_c, 2, D), -6.0, -1.0),
        "rw_w1": nrm((n_c, 2, D, RW_DECAY_LORA), D ** -0.5),
        "rw_w2": nrm((n_c, 2, RW_DECAY_LORA, D), 0.1),
        "rw_a0": nrm((n_c, 2, D), 0.1),
        "rw_a1": nrm((n_c, 2, D, RW_AAA_LORA), D ** -0.5),
        "rw_a2": nrm((n_c, 2, RW_AAA_LORA, D), 0.1),
        "rw_g1": nrm((n_c, 2, D, RW_GATE_LORA), D ** -0.5),
        "rw_g2": nrm((n_c, 2, RW_GATE_LORA, D), RW_GATE_LORA ** -0.5),
        "rw_kk": 0.85 + nrm((n_c, 2, D), 0.02),
        "rw_ka": 1.0 + nrm((n_c, 2, D), 0.02),
        "rw_rk": nrm((n_c, 2, RW_HEADS, RW_HEAD), 0.1),
        "rw_ln_g": 1.0 + nrm((n_c, 2, D), 0.02),
        "rw_ln_b": nrm((n_c, 2, D), 0.02),
    }


def reference(x, c, ctx, c_ctx, ada_w, ada_b, ffn_wg, ffn_wu, ffn_wd, final_g,
              da_wqkv, da_wo, da_lam, da_subln, gq_wqkv, gq_wo, gq_qk_g,
              rw_mu, rw_wr, rw_wk, rw_wv, rw_wo, rw_w0, rw_w1, rw_w2, rw_a0, rw_a1, rw_a2,
              rw_g1, rw_g2, rw_kk, rw_ka, rw_rk, rw_ln_g, rw_ln_b):
    h, hc = x, ctx
    c_act = jax.nn.silu(c)
    cc_act = jax.nn.silu(c_ctx)
    for i in range(DEPTH):
        kind = i % N_MIXERS
        j = i // N_MIXERS
        need_ctx = i < DEPTH - 1
        mod = jnp.split((c_act @ ada_w[i] + ada_b[i])[:, None, :], N_MOD, axis=-1)
        modc = jnp.split(cc_act @ ada_w[i] + ada_b[i], N_MOD, axis=-1)
        h = h + 0.5 * mod[2] * swiglu(modulate(h, mod[0], mod[1]), ffn_wg[i, 0], ffn_wu[i, 0], ffn_wd[i, 0])
        hc = hc + 0.5 * modc[2] * swiglu(modulate(hc, modc[0], modc[1]), ffn_wg[i, 0], ffn_wu[i, 0], ffn_wd[i, 0])
        u = modulate(h, mod[3], mod[4])
        uc = modulate(hc, modc[3], modc[4])
        if kind == 0:
            lam_init = 0.8 - 0.6 * math.exp(-0.3 * i)
            y, yc = differential_attention(u, uc, da_wqkv[j], da_wo[j], da_lam[j], da_subln[j], lam_init, need_ctx)
        elif kind == 1:
            y, yc = grouped_query_attention(u, uc, gq_wqkv[j], gq_wo[j], gq_qk_g[j], need_ctx)
        else:
            y, yc = rwkv7_bidirectional(u, uc, rw_mu[j], rw_wr[j], rw_wk[j], rw_wv[j], rw_wo[j],
                                        rw_w0[j], rw_w1[j], rw_w2[j], rw_a0[j], rw_a1[j], rw_a2[j],
                                        rw_g1[j], rw_g2[j], rw_kk[j], rw_ka[j], rw_rk[j],
                                        rw_ln_g[j], rw_ln_b[j], need_ctx)
        h = h + mod[5] * y
        h = h + 0.5 * mod[8] * swiglu(modulate(h, mod[6], mod[7]), ffn_wg[i, 1], ffn_wu[i, 1], ffn_wd[i, 1])
        if need_ctx:
            hc = hc + modc[5] * yc
            hc = hc + 0.5 * modc[8] * swiglu(modulate(hc, modc[6], modc[7]), ffn_wg[i, 1], ffn_wu[i, 1], ffn_wd[i, 1])
    return rms_norm(h, final_g)
```

```python
import functools
import math

import jax
import jax.numpy as jnp
from jax import lax
from jax.experimental import pallas as pl
from jax.experimental.pallas import tpu as pltpu

F32 = jnp.float32
MXU_DTYPE = jnp.bfloat16
HIGHEST = lax.Precision.HIGHEST

NORM_EPS = 1e-6
ROPE_THETA = 10000.0
GRID_W = 64
N_MOD = 9
DA_HEADS = 8
DA_HEAD_DIM = 64
GQ_HEADS = 8
GQ_KV_HEADS = 2
GQ_HEAD_DIM = 128
GQ_GROUP = GQ_HEADS // GQ_KV_HEADS
RW_HEAD = 64
RW_GN_EPS = 64e-5

LANES = 128
SUBLANES = 8
VMEM_LIMIT_BYTES = 56 * 1024 * 1024

SCAN_CHUNK = 64


def _cparams(n_axes):
    return pltpu.CompilerParams(
        dimension_semantics=("arbitrary",) * n_axes,
        vmem_limit_bytes=VMEM_LIMIT_BYTES,
    )


def _resident(shape):
    zeros = (0,) * len(shape)
    return pl.BlockSpec(shape, lambda *_: zeros)


def _mod_spec(mods, d):
    if mods.shape[0] > 1:
        return pl.BlockSpec((1, N_MOD, d), lambda b, t: (b, 0, 0))
    return pl.BlockSpec((1, N_MOD, d), lambda b, t: (0, 0, 0))


def _rms(x):
    return x * lax.rsqrt(jnp.mean(x * x, axis=-1, keepdims=True) + NORM_EPS)


def _modulate(x, mod_ref, base):
    shift = mod_ref[0, base:base + 1, :]
    scale = mod_ref[0, base + 1:base + 2, :]
    return _rms(x) * (1.0 + scale) + shift


def _dot(a, b):
    return jnp.dot(a, b, preferred_element_type=F32)


def _mx(x):
    return x.astype(MXU_DTYPE)


def _adaln_kernel(c_ref, w_ref, b_ref, o_ref):
    c = c_ref[...]
    act = _mx(c * jax.nn.sigmoid(c))
    o_ref[0] = _dot(act, _mx(w_ref[0])) + b_ref[0]


def _adaln(cvec, ada_w, ada_b):
    depth, d, n = ada_w.shape
    rows = cvec.shape[0]
    tn = 1024
    return pl.pallas_call(
        _adaln_kernel,
        grid=(depth, n // tn),
        in_specs=[
            pl.BlockSpec((rows, d), lambda i, j: (0, 0)),
            pl.BlockSpec((1, d, tn), lambda i, j: (i, 0, j)),
            pl.BlockSpec((1, 1, tn), lambda i, j: (i, 0, j)),
        ],
        out_specs=pl.BlockSpec((1, rows, tn), lambda i, j: (i, 0, j)),
        out_shape=jax.ShapeDtypeStruct((depth, rows, n), F32),
        compiler_params=_cparams(2),
        name="adaln",
    )(cvec, ada_w, ada_b.reshape(depth, 1, n))


def _ffn_kernel(h_ref, mod_ref, wg_ref, wu_ref, wd_ref, o_ref, *, base, tf):
    x = h_ref[0]
    xn = _mx(_modulate(x, mod_ref, base))
    f = wg_ref.shape[1]
    acc = jnp.zeros(x.shape, F32)
    for c in range(f // tf):
        g = _dot(xn, wg_ref[:, c * tf:(c + 1) * tf])
        u = _dot(xn, wu_ref[:, c * tf:(c + 1) * tf])
        a = _mx(g * jax.nn.sigmoid(g) * u)
        acc = acc + _dot(a, wd_ref[c * tf:(c + 1) * tf, :])
    gate = mod_ref[0, base + 2:base + 3, :]
    o_ref[0] = x + (0.5 * gate) * acc


def _ffn(h, mods, wg, wu, wd, base, tm=512, tf=256):
    b, t, d = h.shape
    f = wg.shape[1]
    tm = min(tm, t)
    return pl.pallas_call(
        functools.partial(_ffn_kernel, base=base, tf=tf),
        grid=(b, t // tm),
        in_specs=[
            pl.BlockSpec((1, tm, d), lambda i, j: (i, j, 0)),
            _mod_spec(mods, d),
            _resident((d, f)), _resident((d, f)), _resident((f, d)),
        ],
        out_specs=pl.BlockSpec((1, tm, d), lambda i, j: (i, j, 0)),
        out_shape=jax.ShapeDtypeStruct((b, t, d), F32),
        compiler_params=_cparams(2),
        name="ffn",
    )(h, mods, wg, wu, wd)


def _rope_tables(n_tokens, head_dim):
    rows = n_tokens // GRID_W
    row = jnp.repeat(jnp.arange(rows, dtype=jnp.int32), GRID_W).astype(F32)
    col = jnp.tile(jnp.arange(GRID_W, dtype=jnp.int32), rows).astype(F32)
    n_freq = head_dim // 4
    inv = jnp.power(ROPE_THETA, -jnp.arange(n_freq, dtype=F32) / n_freq)
    ang = jnp.concatenate([row[:, None] * inv, col[:, None] * inv], axis=-1)
    cos, sin = jnp.cos(ang), jnp.sin(ang)
    reps = LANES // head_dim
    cos_t = jnp.tile(jnp.concatenate([cos, cos], axis=-1), (1, reps))
    sin_t = jnp.tile(jnp.concatenate([-sin, sin], axis=-1), (1, reps))
    return cos_t, sin_t


def _rope_slab(x, cos, sin, half):
    if 2 * half == LANES:
        swapped = pltpu.roll(x, half, 1)
    else:
        lane = lax.broadcasted_iota(jnp.int32, x.shape, 1)
        first = (lane & (2 * half - 1)) < half
        swapped = jnp.where(first, pltpu.roll(x, LANES - half, 1), pltpu.roll(x, half, 1))
    return x * cos + swapped * sin


def _da_proj_kernel(h_ref, mod_ref, w_ref, *rest, rope):
    if rope:
        cos_ref, sin_ref, q_ref, k_ref, v_ref = rest
        cos, sin = cos_ref[...], sin_ref[...]
    else:
        q_ref, k_ref, v_ref = rest
    d = h_ref.shape[2]
    u = _mx(_modulate(h_ref[0], mod_ref, 3))
    scale = DA_HEAD_DIM ** -0.5
    for idx, (o_ref, rot, mul) in enumerate(((q_ref, True, scale), (k_ref, True, 1.0), (v_ref, False, 1.0))):
        y = _dot(u, w_ref[:, idx * d:(idx + 1) * d])
        for j in range(d // LANES):
            ys = y[:, j * LANES:(j + 1) * LANES]
            if rot and rope:
                ys = _rope_slab(ys, cos, sin, DA_HEAD_DIM // 2)
            if mul != 1.0:
                ys = ys * mul
            o_ref[0, :, j * LANES:(j + 1) * LANES] = ys.astype(o_ref.dtype)


def _da_proj(h, mods, wqkv, tables, tm=512):
    b, t, d = h.shape
    tm = min(tm, t)
    rope = tables is not None
    tok = pl.BlockSpec((1, tm, d), lambda i, j: (i, j, 0))
    in_specs = [tok, _mod_spec(mods, d), _resident(wqkv.shape)]
    args = [h, mods, wqkv]
    if rope:
        tab = pl.BlockSpec((tm, LANES), lambda i, j: (j, 0))
        in_specs += [tab, tab]
        args += list(tables)
    out = jax.ShapeDtypeStruct((b, t, d), MXU_DTYPE)
    return pl.pallas_call(
        functools.partial(_da_proj_kernel, rope=rope),
        grid=(b, t // tm),
        in_specs=in_specs,
        out_specs=[tok, tok, tok],
        out_shape=[out, out, out],
        compiler_params=_cparams(2),
        name="da_proj",
    )(*args)


def _gq_proj_kernel(h_ref, mod_ref, w_ref, g_ref, *rest, rope):
    if rope:
        cos_ref, sin_ref, q_ref, k_ref, v_ref = rest
        cos, sin = cos_ref[...], sin_ref[...]
    else:
        q_ref, k_ref, v_ref = rest
    u = _mx(_modulate(h_ref[0], mod_ref, 3))
    nq = GQ_HEADS * GQ_HEAD_DIM
    nk = GQ_KV_HEADS * GQ_HEAD_DIM
    scale = GQ_HEAD_DIM ** -0.5
    plan = ((q_ref, 0, nq, 0, scale), (k_ref, nq, nk, 1, 1.0), (v_ref, nq + nk, nk, None, 1.0))
    for o_ref, start, width, gain_row, mul in plan:
        y = _dot(u, w_ref[:, start:start + width])
        for j in range(width // LANES):
            ys = y[:, j * LANES:(j + 1) * LANES]
            if gain_row is not None:
                ys = _rms(ys) * g_ref[gain_row:gain_row + 1, :]
                if rope:
                    ys = _rope_slab(ys, cos, sin, GQ_HEAD_DIM // 2)
            if mul != 1.0:
                ys = ys * mul
            o_ref[0, :, j * LANES:(j + 1) * LANES] = ys.astype(o_ref.dtype)


def _gq_proj(h, mods, wqkv, qk_g, tables, tm=512):
    b, t, d = h.shape
    tm = min(tm, t)
    rope = tables is not None
    nq = GQ_HEADS * GQ_HEAD_DIM
    nk = GQ_KV_HEADS * GQ_HEAD_DIM
    tok = pl.BlockSpec((1, tm, d), lambda i, j: (i, j, 0))
    in_specs = [tok, _mod_spec(mods, d), _resident(wqkv.shape), _resident(qk_g.shape)]
    args = [h, mods, wqkv, qk_g]
    if rope:
        tab = pl.BlockSpec((tm, LANES), lambda i, j: (j, 0))
        in_specs += [tab, tab]
        args += list(tables)
    return pl.pallas_call(
        functools.partial(_gq_proj_kernel, rope=rope),
        grid=(b, t // tm),
        in_specs=in_specs,
        out_specs=[pl.BlockSpec((1, tm, nq), lambda i, j: (i, j, 0)),
                   pl.BlockSpec((1, tm, nk), lambda i, j: (i, j, 0)),
                   pl.BlockSpec((1, tm, nk), lambda i, j: (i, j, 0))],
        out_shape=[jax.ShapeDtypeStruct((b, t, nq), MXU_DTYPE),
                   jax.ShapeDtypeStruct((b, t, nk), MXU_DTYPE),
                   jax.ShapeDtypeStruct((b, t, nk), MXU_DTYPE)],
        compiler_params=_cparams(2),
        name="gq_proj",
    )(*args)


def _flash(qs, kv_refs, tk, m_ref, l_ref, acc_ref):
    m_ref[...] = jnp.full(m_ref.shape, -jnp.inf, F32)
    l_ref[...] = jnp.zeros(l_ref.shape, F32)
    acc_ref[...] = jnp.zeros(acc_ref.shape, F32)
    for k_ref, v_ref in kv_refs:
        n = k_ref.shape[1]
        step = min(tk, n)

        def body(c, carry, k_ref=k_ref, v_ref=v_ref, step=step):
            start = pl.multiple_of(c * step, step)
            kc = k_ref[0, pl.ds(start, step), :]
            vc = v_ref[0, pl.ds(start, step), :]
            s = lax.dot_general(qs, kc, (((1,), (1,)), ((), ())), preferred_element_type=F32)
            m_prev = m_ref[...]
            m_new = jnp.maximum(m_prev, jnp.max(s, axis=-1, keepdims=True))
            alpha = jnp.exp(m_prev - m_new)
            p = jnp.exp(s - m_new)
            l_ref[...] = alpha * l_ref[...] + jnp.sum(p, axis=-1, keepdims=True)
            acc_ref[...] = alpha * acc_ref[...] + _dot(_mx(p), vc)
            m_ref[...] = m_new
            return carry

        lax.fori_loop(0, n // step, body, 0)


def _da_attn_kernel(lam_ref, g_ref, q_ref, *rest, nseg, tk, lam_init):
    kv_refs = [(rest[2 * i], rest[2 * i + 1]) for i in range(nseg)]
    o_ref, m_ref, l_ref, acc_ref = rest[2 * nseg:]
    q = q_ref[0]
    tq = q.shape[0]
    lane = lax.broadcasted_iota(jnp.int32, q.shape, 1)
    zero = jnp.zeros_like(q)
    qs = jnp.concatenate([jnp.where(lane < DA_HEAD_DIM, q, zero), jnp.where(lane >= DA_HEAD_DIM, q, zero)], axis=0)
    _flash(qs, kv_refs, tk, m_ref, l_ref, acc_ref)
    o = acc_ref[...] / l_ref[...]
    lv = lam_ref[...]
    lam = (jnp.exp(jnp.sum(lv[0:1] * lv[1:2], keepdims=True))
           - jnp.exp(jnp.sum(lv[2:3] * lv[3:4], keepdims=True)) + lam_init)
    od = o[:tq] - lam * o[tq:]
    y = _rms(od) * g_ref[...] * (1.0 - lam_init)
    o_ref[0] = y.astype(o_ref.dtype)


def _da_attn(q, kv_segs, lam_vecs, subln, lam_init, tq=128, tk=512):
    b, t, d = q.shape
    tq = min(tq, t)
    in_specs = [_resident(lam_vecs.shape), _resident(subln.shape),
                pl.BlockSpec((1, tq, LANES), lambda i, h, j: (i, j, h))]
    args = [lam_vecs, subln, q]
    for k, v in kv_segs:
        n = k.shape[1]
        seg = pl.BlockSpec((1, n, LANES), lambda i, h, j: (i, 0, h))
        in_specs += [seg, seg]
        args += [k, v]
    return pl.pallas_call(
        functools.partial(_da_attn_kernel, nseg=len(kv_segs), tk=tk, lam_init=lam_init),
        grid=(b, DA_HEADS, t // tq),
        in_specs=in_specs,
        out_specs=pl.BlockSpec((1, tq, LANES), lambda i, h, j: (i, j, h)),
        out_shape=jax.ShapeDtypeStruct((b, t, d), MXU_DTYPE),
        scratch_shapes=[pltpu.VMEM((2 * tq, 1), F32), pltpu.VMEM((2 * tq, 1), F32),
                        pltpu.VMEM((2 * tq, LANES), F32)],
        compiler_params=_cparams(3),
        name="da_attn",
    )(*args)


def _gq_attn_kernel(q_ref, *rest, nseg, tk):
    kv_refs = [(rest[2 * i], rest[2 * i + 1]) for i in range(nseg)]
    o_ref, m_ref, l_ref, acc_ref = rest[2 * nseg:]
    tq = q_ref.shape[1]
    qs = jnp.concatenate([q_ref[0, :, m * LANES:(m + 1) * LANES] for m in range(GQ_GROUP)], axis=0)
    _flash(qs, kv_refs, tk, m_ref, l_ref, acc_ref)
    o = acc_ref[...] / l_ref[...]
    for m in range(GQ_GROUP):
        o_ref[0, :, m * LANES:(m + 1) * LANES] = o[m * tq:(m + 1) * tq].astype(o_ref.dtype)


def _gq_attn(q, kv_segs, tq=128, tk=512):
    b, t, d = q.shape
    tq = min(tq, t)
    gw = GQ_GROUP * GQ_HEAD_DIM
    in_specs = [pl.BlockSpec((1, tq, gw), lambda i, g, j: (i, j, g))]
    args = [q]
    for k, v in kv_segs:
        n = k.shape[1]
        seg = pl.BlockSpec((1, n, LANES), lambda i, g, j: (i, 0, g))
        in_specs += [seg, seg]
        args += [k, v]
    rows = GQ_GROUP * tq
    return pl.pallas_call(
        functools.partial(_gq_attn_kernel, nseg=len(kv_segs), tk=tk),
        grid=(b, GQ_KV_HEADS, t // tq),
        in_specs=in_specs,
        out_specs=pl.BlockSpec((1, tq, gw), lambda i, g, j: (i, j, g)),
        out_shape=jax.ShapeDtypeStruct((b, t, d), MXU_DTYPE),
        scratch_shapes=[pltpu.VMEM((rows, 1), F32), pltpu.VMEM((rows, 1), F32),
                        pltpu.VMEM((rows, LANES), F32)],
        compiler_params=_cparams(3),
        name="gq_attn",
    )(*args)


def _out_proj_kernel(h_ref, mod_ref, y_ref, w_ref, o_ref):
    gate = mod_ref[0, 5:6, :]
    o_ref[0] = h_ref[0] + gate * _dot(y_ref[0], w_ref[...])


def _out_proj(h, mods, y, wo, tm=512):
    b, t, d = h.shape
    tm = min(tm, t)
    tok = pl.BlockSpec((1, tm, d), lambda i, j: (i, j, 0))
    return pl.pallas_call(
        _out_proj_kernel,
        grid=(b, t // tm),
        in_specs=[tok, _mod_spec(mods, d), tok, _resident(wo.shape)],
        out_specs=tok,
        out_shape=jax.ShapeDtypeStruct((b, t, d), F32),
        compiler_params=_cparams(2),
        name="out_proj",
    )(h, mods, y, wo)


def _head_sum(x, e_ref):
    hi = _mx(x)
    lo = _mx(x - hi.astype(F32))
    return _dot(hi, e_ref[...]) + _dot(lo, e_ref[...])


def _rw_tok_kernel(h_ref, hp_ref, hn_ref, mod_ref, mu_ref, wr_ref, wk_ref, wv_ref,
                   w0_ref, w1_ref, w2_ref, a0_ref, a1_ref, a2_ref, g1_ref, g2_ref, kk_ref, ka_ref, e_ref,
                   r_out, v_out, lw_out, kd_out, kn_out, ba_out, g_out):
    t = pl.program_id(1)
    nt = pl.num_programs(1)
    tm = h_ref.shape[1]
    z = _modulate(h_ref[0], mod_ref, 3)
    zp = _modulate(hp_ref[0], mod_ref, 3)[SUBLANES - 1:SUBLANES]
    zn = _modulate(hn_ref[0], mod_ref, 3)[0:1]
    zp = jnp.where(t > 0, zp, jnp.zeros_like(zp))
    zn = jnp.where(t < nt - 1, zn, jnp.zeros_like(zn))
    row = lax.broadcasted_iota(jnp.int32, z.shape, 0)
    dp = jnp.where(row == 0, zp, pltpu.roll(z, 1, 0)) - z
    dn = jnp.where(row == tm - 1, zn, pltpu.roll(z, tm - 1, 0)) - z

    def mix(i):
        return z + dp * mu_ref[0, i:i + 1, :] + dn * mu_ref[1, i:i + 1, :]

    r = _dot(_mx(mix(0)), wr_ref[...])
    k = _dot(_mx(mix(2)), wk_ref[...])
    v = _dot(_mx(mix(3)), wv_ref[...])
    xw, xa, xg = _mx(mix(1)), _mx(mix(4)), _mx(mix(5))
    r_out[0] = r
    v_out[0] = v
    for d in range(2):
        x = -(w0_ref[d:d + 1, :] + _dot(_mx(jnp.tanh(_dot(xw, w1_ref[d]))), w2_ref[d]))
        softplus = jnp.maximum(x, 0.0) + jnp.log(1.0 + jnp.exp(-jnp.abs(x)))
        lw_out[d, 0] = -jnp.exp(-softplus - 0.5)
        a = jax.nn.sigmoid(a0_ref[d:d + 1, :] + _dot(_mx(_dot(xa, a1_ref[d])), a2_ref[d]))
        g_out[d, 0] = _dot(_mx(jax.nn.sigmoid(_dot(xg, g1_ref[d]))), g2_ref[d])
        kx = k * kk_ref[d:d + 1, :]
        kn = kx / jnp.maximum(jnp.sqrt(_head_sum(kx * kx, e_ref)), 1e-12)
        kn_out[d, 0] = kn
        ba_out[d, 0] = kn * a
        kd_out[d, 0] = k * (1.0 + (a - 1.0) * ka_ref[d:d + 1, :])


def _rw_tok(h, mods, p, tm=128):
    b, t, d = h.shape
    tm = min(tm, t)
    nb8 = t // SUBLANES
    per8 = tm // SUBLANES
    tok = pl.BlockSpec((1, tm, d), lambda i, j: (i, j, 0))
    prev = pl.BlockSpec((1, SUBLANES, d), lambda i, j: (i, jnp.maximum(j * per8 - 1, 0), 0))
    nxt = pl.BlockSpec((1, SUBLANES, d), lambda i, j: (i, jnp.minimum((j + 1) * per8, nb8 - 1), 0))
    names = ("mu", "wr", "wk", "wv", "w0", "w1", "w2", "a0", "a1", "a2", "g1", "g2", "kk", "ka", "e")
    weights = [p[n] for n in names]
    both = pl.BlockSpec((2, 1, tm, d), lambda i, j: (0, i, j, 0))
    one = jax.ShapeDtypeStruct((b, t, d), F32)
    two = jax.ShapeDtypeStruct((2, b, t, d), F32)
    return pl.pallas_call(
        _rw_tok_kernel,
        grid=(b, t // tm),
        in_specs=[tok, prev, nxt, _mod_spec(mods, d)] + [_resident(w.shape) for w in weights],
        out_specs=[tok, tok, both, both, both, both, both],
        out_shape=[one, one, two, two, two, two, two],
        compiler_params=_cparams(2),
        name="rw_tok",
    )(h, h, h, mods, *weights)


def _mmf(a, b, dims=(((1,), (0,)), ((), ()))):
    return lax.dot_general(a, b, dims, precision=HIGHEST, preferred_element_type=F32)


_NT = (((1,), (1,)), ((), ()))
_TN = (((0,), (0,)), ((), ()))


def _scan_kernel(s0_ref, lw_ref, kn_ref, ba_ref, kd_ref, r_ref, v_ref, y_ref, sf_ref, s_scr, *, rev):
    L = SCAN_CHUNK
    t = pl.program_id(1)
    n_pairs = s_scr.shape[0]
    n_chunks = lw_ref.shape[2] // L

    @pl.when(t == 0)
    def _():
        s_scr[...] = s0_ref[0]

    r1 = lax.broadcasted_iota(jnp.int32, (L, LANES), 0)
    c1 = lax.broadcasted_iota(jnp.int32, (L, LANES), 1) & (L - 1)
    incl = (c1 >= r1) if rev else (c1 <= r1)
    strict = (c1 > r1) if rev else (c1 < r1)
    tr = lax.broadcasted_iota(jnp.int32, (L, L), 0)
    tc = lax.broadcasted_iota(jnp.int32, (L, L), 1)
    tri = jnp.where((tc >= tr) if rev else (tc <= tr), 1.0, 0.0).astype(F32)
    r2 = lax.broadcasted_iota(jnp.int32, (2 * L, LANES), 0)
    c2 = lax.broadcasted_iota(jnp.int32, (2 * L, LANES), 1)
    own = (r2 < L) == (c2 < L)
    eye = jnp.where(r2 == c2, 1.0, 0.0).astype(F32)

    def bd(x):
        return jnp.where(own, jnp.concatenate([x, x], axis=0), 0.0)

    def chunk(ci, carry):
        cc = (n_chunks - 1 - ci) if rev else ci
        rows = pl.ds(pl.multiple_of(cc * L, L), L)
        for p in range(n_pairs):
            lanes = slice(p * LANES, (p + 1) * LANES)
            lw = lw_ref[0, 0, rows, lanes]
            kn = kn_ref[0, 0, rows, lanes]
            ba = ba_ref[0, 0, rows, lanes]
            kd = kd_ref[0, 0, rows, lanes]
            r = r_ref[0, rows, lanes]
            v = v_ref[0, rows, lanes]
            s_in = s_scr[p]

            cin = _mmf(tri, lw)
            e_in = jnp.exp(cin)
            e_neg = jnp.exp(-cin)
            at = -kn * jnp.exp(cin - lw)
            rt = r * e_in
            bt = ba * e_neg
            kt = kd * e_neg
            wl = e_in[0:1] if rev else e_in[L - 1:L]
            ar = jnp.concatenate([at, rt], axis=0)
            gb = _mmf(ar, bd(bt), _NT)
            gk = _mmf(ar, bd(kt), _NT)
            aab = jnp.where(strict, gb[:L], 0.0)
            arb = bd(jnp.where(incl, gb[L:], 0.0))
            aak = bd(jnp.where(strict, gk[:L], 0.0))
            ark = bd(jnp.where(incl, gk[L:], 0.0))
            pw = bd(aab)
            tinv = eye + pw
            for _ in range(int(math.log2(L)) - 1):
                pw = _mmf(pw, pw)
                tinv = tinv + _mmf(pw, tinv)
            v_sd = bd(v)
            pm = _mmf(tinv, bd(at))
            qm = _mmf(tinv, _mmf(aak, v_sd))
            bh = bd(bt * wl)
            kh = bd(kt * wl)
            m_mat = eye * wl + _mmf(pm, bh, _TN)
            c_mat = _mmf(qm, bh, _TN) + _mmf(v_sd, kh, _TN)
            ry = bd(rt) + _mmf(arb, pm)
            y0 = _mmf(arb, qm) + _mmf(ark, v_sd)
            ysd = _mmf(ry, s_in, _NT) + y0
            y_ref[0, rows, lanes] = ysd[:L] + ysd[L:]
            s_scr[p] = _mmf(s_in, m_mat) + c_mat
        return carry

    lax.fori_loop(0, n_chunks, chunk, 0)
    sf_ref[0] = s_scr[...]


def _scan(s0, lw, kn, ba, kd, r, v, direction, tb=256):
    b, t, d = r.shape
    tb = min(tb, t)
    nb = t // tb
    rev = direction == 1
    n_pairs = d // LANES

    def tmap(j):
        return (nb - 1 - j) if rev else j

    dirspec = pl.BlockSpec((1, 1, tb, d), lambda i, j: (direction, i, tmap(j), 0))
    tok = pl.BlockSpec((1, tb, d), lambda i, j: (i, tmap(j), 0))
    st = pl.BlockSpec((1, n_pairs, LANES, LANES), lambda i, j: (i, 0, 0, 0))
    return pl.pallas_call(
        functools.partial(_scan_kernel, rev=rev),
        grid=(b, nb),
        in_specs=[st, dirspec, dirspec, dirspec, dirspec, tok, tok],
        out_specs=[tok, st],
        out_shape=[jax.ShapeDtypeStruct((b, t, d), F32),
                   jax.ShapeDtypeStruct((b, n_pairs, LANES, LANES), F32)],
        scratch_shapes=[pltpu.VMEM((n_pairs, LANES, LANES), F32)],
        compiler_params=_cparams(2),
        name="rw_scan_rev" if rev else "rw_scan_fwd",
    )(s0, lw, kn, ba, kd, r, v)


def _rw_out_kernel(h_ref, mod_ref, y0_ref, y1_ref, kd_ref, g_ref, r_ref, v_ref,
                   rk_ref, lng_ref, lnb_ref, e_ref, wo_ref, o_ref):
    r = r_ref[0]
    v = v_ref[0]
    inv_n = 1.0 / RW_HEAD
    total = jnp.zeros(r.shape, F32)
    for d, y_ref in enumerate((y0_ref, y1_ref)):
        y = y_ref[0]
        yc = y - _head_sum(y, e_ref) * inv_n
        var = _head_sum(yc * yc, e_ref) * inv_n
        yn = yc * lax.rsqrt(var + RW_GN_EPS)
        bonus = _head_sum(r * kd_ref[d, 0] * rk_ref[d:d + 1, :], e_ref) * v
        out = yn * lng_ref[d:d + 1, :] + lnb_ref[d:d + 1, :] + bonus
        total = total + out * g_ref[d, 0]
    gate = mod_ref[0, 5:6, :]
    o_ref[0] = h_ref[0] + gate * _dot(_mx(total), wo_ref[...])


def _rw_out(h, mods, y0, y1, kd, g, r, v, p, tm=256):
    b, t, d = h.shape
    tm = min(tm, t)
    tok = pl.BlockSpec((1, tm, d), lambda i, j: (i, j, 0))
    both = pl.BlockSpec((2, 1, tm, d), lambda i, j: (0, i, j, 0))
    weights = [p["rk"], p["ln_g"], p["ln_b"], p["e"], p["wo"]]
    return pl.pallas_call(
        _rw_out_kernel,
        grid=(b, t // tm),
        in_specs=[tok, _mod_spec(mods, d), tok, tok, both, both, tok, tok] + [_resident(w.shape) for w in weights],
        out_specs=tok,
        out_shape=jax.ShapeDtypeStruct((b, t, d), F32),
        compiler_params=_cparams(2),
        name="rw_out",
    )(h, mods, y0, y1, kd, g, r, v, *weights)


def _final_norm_kernel(h_ref, g_ref, o_ref):
    o_ref[0] = _rms(h_ref[0]) * g_ref[...]


def _final_norm(h, gain, tm=512):
    b, t, d = h.shape
    tm = min(tm, t)
    tok = pl.BlockSpec((1, tm, d), lambda i, j: (i, j, 0))
    return pl.pallas_call(
        _final_norm_kernel,
        grid=(b, t // tm),
        in_specs=[tok, _resident((1, d))],
        out_specs=tok,
        out_shape=jax.ShapeDtypeStruct((b, t, d), F32),
        compiler_params=_cparams(2),
        name="final_norm",
    )(h, gain.reshape(1, d))


def _da_layer(h, hc, mod, modc, wqkv, wo, lam_vecs, subln, lam_init, need_ctx):
    s = h.shape[1]
    wqkv, wo = _mx(wqkv), _mx(wo)
    subln = subln.reshape(1, -1)
    q, k, v = _da_proj(h, mod, wqkv, _rope_tables(s, DA_HEAD_DIM))
    qc, kc, vc = _da_proj(hc, modc, wqkv, None)
    y = _da_attn(q, [(kc, vc), (k, v)], lam_vecs, subln, lam_init)
    h = _out_proj(h, mod, y, wo)
    if need_ctx:
        yc = _da_attn(qc, [(kc, vc)], lam_vecs, subln, lam_init)
        hc = _out_proj(hc, modc, yc, wo)
    return h, hc


def _gq_layer(h, hc, mod, modc, wqkv, wo, qk_g, need_ctx):
    s = h.shape[1]
    wqkv, wo = _mx(wqkv), _mx(wo)
    q, k, v = _gq_proj(h, mod, wqkv, qk_g, _rope_tables(s, GQ_HEAD_DIM))
    qc, kc, vc = _gq_proj(hc, modc, wqkv, qk_g, None)
    y = _gq_attn(q, [(kc, vc), (k, v)])
    h = _out_proj(h, mod, y, wo)
    if need_ctx:
        yc = _gq_attn(qc, [(kc, vc)])
        hc = _out_proj(hc, modc, yc, wo)
    return h, hc


def _rw_layer(h, hc, mod, modc, p, need_ctx):
    b, _, d = h.shape
    group = lax.broadcasted_iota(jnp.int32, (d, d), 0) // RW_HEAD == lax.broadcasted_iota(jnp.int32, (d, d), 1) // RW_HEAD
    p = dict(p, e=group.astype(MXU_DTYPE))
    r, v, lw, kd, kn, ba, g = _rw_tok(h, mod, p)
    rc, vc, lwc, kdc, knc, bac, gc = _rw_tok(hc, modc, p)
    zero = jnp.zeros((b, d // LANES, LANES, LANES), F32)
    ys, ycs = [], []
    for direction in range(2):
        yc, state_c = _scan(zero, lwc, knc, bac, kdc, rc, vc, direction)
        yl, _ = _scan(state_c, lw, kn, ba, kd, r, v, direction)
        ys.append(yl)
        ycs.append(yc)
    h = _rw_out(h, mod, ys[0], ys[1], kd, g, r, v, p)
    if need_ctx:
        hc = _rw_out(hc, modc, ycs[0], ycs[1], kdc, gc, rc, vc, p)
    return h, hc


def kernel(x, c, ctx, c_ctx, ada_w, ada_b, ffn_wg, ffn_wu, ffn_wd, final_g, da_wqkv, da_wo, da_lam, da_subln, gq_wqkv, gq_wo, gq_qk_g, rw_mu, rw_wr, rw_wk, rw_wv, rw_wo, rw_w0, rw_w1, rw_w2, rw_a0, rw_a1, rw_a2, rw_g1, rw_g2, rw_kk, rw_ka, rw_rk, rw_ln_g, rw_ln_b):
    b, _, d = x.shape
    depth = ada_w.shape[0]
    rows = -(-(b + 1) // SUBLANES) * SUBLANES
    cvec = jnp.zeros((rows, d), F32).at[:b].set(c).at[b].set(c_ctx)
    mods = _adaln(cvec, ada_w, ada_b)
    h, hc = x, ctx
    for i in range(depth):
        kind, j = i % 3, i // 3
        need_ctx = i < depth - 1
        mod = mods[i, :b].reshape(b, N_MOD, d)
        modc = mods[i, b:b + 1].reshape(1, N_MOD, d)
        wg, wu, wd = _mx(ffn_wg[i, 0]), _mx(ffn_wu[i, 0]), _mx(ffn_wd[i, 0])
        h = _ffn(h, mod, wg, wu, wd, 0)
        hc = _ffn(hc, modc, wg, wu, wd, 0)
        if kind == 0:
            lam_init = 0.8 - 0.6 * math.exp(-0.3 * i)
            h, hc = _da_layer(h, hc, mod, modc, da_wqkv[j], da_wo[j], da_lam[j], da_subln[j], lam_init, need_ctx)
        elif kind == 1:
            h, hc = _gq_layer(h, hc, mod, modc, gq_wqkv[j], gq_wo[j], gq_qk_g[j], need_ctx)
        else:
            p = dict(mu=rw_mu[j], wr=_mx(rw_wr[j]), wk=_mx(rw_wk[j]), wv=_mx(rw_wv[j]), wo=_mx(rw_wo[j]),
                     w0=rw_w0[j], w1=_mx(rw_w1[j]), w2=_mx(rw_w2[j]), a0=rw_a0[j], a1=_mx(rw_a1[j]),
                     a2=_mx(rw_a2[j]), g1=_mx(rw_g1[j]), g2=_mx(rw_g2[j]), kk=rw_kk[j], ka=rw_ka[j],
                     rk=rw_rk[j].reshape(2, d), ln_g=rw_ln_g[j], ln_b=rw_ln_b[j])
            h, hc = _rw_layer(h, hc, mod, modc, p, need_ctx)
        wg, wu, wd = _mx(ffn_wg[i, 1]), _mx(ffn_wu[i, 1]), _mx(ffn_wd[i, 1])
        h = _ffn(h, mod, wg, wu, wd, 6)
        if need_ctx:
            hc = _ffn(hc, modc, wg, wu, wd, 6)
    return _final_norm(h, final_g)
```

```python
import functools
import math

import jax
import jax.numpy as jnp
from jax import lax
from jax.experimental import pallas as pl
from jax.experimental.pallas import tpu as pltpu

F32 = jnp.float32
MXU_DTYPE = jnp.bfloat16

NORM_EPS = 1e-6
ROPE_THETA = 10000.0
GRID_W = 64
N_MOD = 9
DA_HEADS = 8
DA_HEAD_DIM = 64
GQ_HEADS = 8
GQ_KV_HEADS = 2
GQ_HEAD_DIM = 128
GQ_GROUP = GQ_HEADS // GQ_KV_HEADS
RW_HEAD = 64
RW_GN_EPS = 64e-5

LANES = 128
SUBLANES = 8
VMEM_LIMIT_BYTES = 56 * 1024 * 1024

SCAN_CHUNK = 64


def _cparams(n_axes):
    return pltpu.CompilerParams(
        dimension_semantics=("arbitrary",) * n_axes,
        vmem_limit_bytes=VMEM_LIMIT_BYTES,
    )


def _resident(shape):
    zeros = (0,) * len(shape)
    return pl.BlockSpec(shape, lambda *_: zeros)


def _mod_spec(mods, d):
    if mods.shape[0] > 1:
        return pl.BlockSpec((1, N_MOD, d), lambda b, t: (b, 0, 0))
    return pl.BlockSpec((1, N_MOD, d), lambda b, t: (0, 0, 0))


def _rms(x):
    return x * lax.rsqrt(jnp.mean(x * x, axis=-1, keepdims=True) + NORM_EPS)


def _modulate(x, mod_ref, base):
    shift = mod_ref[0, base:base + 1, :]
    scale = mod_ref[0, base + 1:base + 2, :]
    return _rms(x) * (1.0 + scale) + shift


def _dot(a, b):
    return jnp.dot(a, b, preferred_element_type=F32)


def _mx(x):
    return x.astype(MXU_DTYPE)


_NT = (((1,), (1,)), ((), ()))
_TN = (((0,), (0,)), ((), ()))
_NN = (((1,), (0,)), ((), ()))
LOG2E = math.log2(math.e)


def _adaln_kernel(c_ref, w_ref, b_ref, o_ref):
    c = c_ref[...]
    act = _mx(c * jax.nn.sigmoid(c))
    o_ref[0] = _dot(act, _mx(w_ref[0])) + b_ref[0]


def _adaln(cvec, ada_w, ada_b):
    depth, d, n = ada_w.shape
    rows = cvec.shape[0]
    tn = 1024
    return pl.pallas_call(
        _adaln_kernel,
        grid=(depth, n // tn),
        in_specs=[
            pl.BlockSpec((rows, d), lambda i, j: (0, 0)),
            pl.BlockSpec((1, d, tn), lambda i, j: (i, 0, j)),
            pl.BlockSpec((1, 1, tn), lambda i, j: (i, 0, j)),
        ],
        out_specs=pl.BlockSpec((1, rows, tn), lambda i, j: (i, 0, j)),
        out_shape=jax.ShapeDtypeStruct((depth, rows, n), F32),
        compiler_params=_cparams(2),
        name="adaln",
    )(cvec, ada_w, ada_b.reshape(depth, 1, n))


def _ffn_kernel(h_ref, mod_ref, wg_ref, wu_ref, wd_ref, o_ref, *, base, tf):
    x = h_ref[0]
    xn = _mx(_modulate(x, mod_ref, base))
    f = wg_ref.shape[1]
    acc = jnp.zeros(x.shape, F32)
    for c in range(f // tf):
        g = _dot(xn, wg_ref[:, c * tf:(c + 1) * tf])
        u = _dot(xn, wu_ref[:, c * tf:(c + 1) * tf])
        a = _mx(g * jax.nn.sigmoid(g) * u)
        acc = acc + _dot(a, wd_ref[c * tf:(c + 1) * tf, :])
    gate = mod_ref[0, base + 2:base + 3, :]
    o_ref[0] = x + (0.5 * gate) * acc


def _ffn(h, mods, wg, wu, wd, base, tm=512, tf=256):
    b, t, d = h.shape
    f = wg.shape[1]
    tm = min(tm, t)
    return pl.pallas_call(
        functools.partial(_ffn_kernel, base=base, tf=tf),
        grid=(b, t // tm),
        in_specs=[
            pl.BlockSpec((1, tm, d), lambda i, j: (i, j, 0)),
            _mod_spec(mods, d),
            _resident((d, f)), _resident((d, f)), _resident((f, d)),
        ],
        out_specs=pl.BlockSpec((1, tm, d), lambda i, j: (i, j, 0)),
        out_shape=jax.ShapeDtypeStruct((b, t, d), F32),
        compiler_params=_cparams(2),
        name="ffn",
    )(h, mods, wg, wu, wd)


def _rope_tables(n_tokens, head_dim):
    rows = n_tokens // GRID_W
    row = jnp.repeat(jnp.arange(rows, dtype=jnp.int32), GRID_W).astype(F32)
    col = jnp.tile(jnp.arange(GRID_W, dtype=jnp.int32), rows).astype(F32)
    n_freq = head_dim // 4
    inv = jnp.power(ROPE_THETA, -jnp.arange(n_freq, dtype=F32) / n_freq)
    ang = jnp.concatenate([row[:, None] * inv, col[:, None] * inv], axis=-1)
    cos, sin = jnp.cos(ang), jnp.sin(ang)
    reps = LANES // head_dim
    cos_t = jnp.tile(jnp.concatenate([cos, cos], axis=-1), (1, reps))
    sin_t = jnp.tile(jnp.concatenate([-sin, sin], axis=-1), (1, reps))
    return cos_t, sin_t


def _rope_slab(x, cos, sin, half):
    if 2 * half == LANES:
        swapped = pltpu.roll(x, half, 1)
    else:
        lane = lax.broadcasted_iota(jnp.int32, x.shape, 1)
        first = (lane & (2 * half - 1)) < half
        swapped = jnp.where(first, pltpu.roll(x, LANES - half, 1), pltpu.roll(x, half, 1))
    return x * cos + swapped * sin


def _da_proj_kernel(h_ref, mod_ref, w_ref, *rest, rope):
    if rope:
        cos_ref, sin_ref, q_ref, k_ref, v_ref = rest
        cos, sin = cos_ref[...], sin_ref[...]
    else:
        q_ref, k_ref, v_ref = rest
    d = h_ref.shape[2]
    u = _mx(_modulate(h_ref[0], mod_ref, 3))
    scale = DA_HEAD_DIM ** -0.5 * LOG2E
    for idx, (o_ref, rot, mul) in enumerate(((q_ref, True, scale), (k_ref, True, 1.0), (v_ref, False, 1.0))):
        y = _dot(u, w_ref[:, idx * d:(idx + 1) * d])
        for j in range(d // LANES):
            ys = y[:, j * LANES:(j + 1) * LANES]
            if rot and rope:
                ys = _rope_slab(ys, cos, sin, DA_HEAD_DIM // 2)
            if mul != 1.0:
                ys = ys * mul
            o_ref[0, :, j * LANES:(j + 1) * LANES] = ys.astype(o_ref.dtype)


def _da_proj(h, mods, wqkv, tables, tm=512):
    b, t, d = h.shape
    tm = min(tm, t)
    rope = tables is not None
    tok = pl.BlockSpec((1, tm, d), lambda i, j: (i, j, 0))
    in_specs = [tok, _mod_spec(mods, d), _resident(wqkv.shape)]
    args = [h, mods, wqkv]
    if rope:
        tab = pl.BlockSpec((tm, LANES), lambda i, j: (j, 0))
        in_specs += [tab, tab]
        args += list(tables)
    out = jax.ShapeDtypeStruct((b, t, d), MXU_DTYPE)
    return pl.pallas_call(
        functools.partial(_da_proj_kernel, rope=rope),
        grid=(b, t // tm),
        in_specs=in_specs,
        out_specs=[tok, tok, tok],
        out_shape=[out, out, out],
        compiler_params=_cparams(2),
        name="da_proj",
    )(*args)


def _gq_proj_kernel(h_ref, mod_ref, w_ref, g_ref, *rest, rope):
    if rope:
        cos_ref, sin_ref, q_ref, k_ref, v_ref = rest
        cos, sin = cos_ref[...], sin_ref[...]
    else:
        q_ref, k_ref, v_ref = rest
    u = _mx(_modulate(h_ref[0], mod_ref, 3))
    nq = GQ_HEADS * GQ_HEAD_DIM
    nk = GQ_KV_HEADS * GQ_HEAD_DIM
    scale = GQ_HEAD_DIM ** -0.5 * LOG2E
    plan = ((q_ref, 0, nq, 0, scale), (k_ref, nq, nk, 1, 1.0), (v_ref, nq + nk, nk, None, 1.0))
    for o_ref, start, width, gain_row, mul in plan:
        y = _dot(u, w_ref[:, start:start + width])
        for j in range(width // LANES):
            ys = y[:, j * LANES:(j + 1) * LANES]
            if gain_row is not None:
                ys = _rms(ys) * g_ref[gain_row:gain_row + 1, :]
                if rope:
                    ys = _rope_slab(ys, cos, sin, GQ_HEAD_DIM // 2)
            if mul != 1.0:
                ys = ys * mul
            o_ref[0, :, j * LANES:(j + 1) * LANES] = ys.astype(o_ref.dtype)


def _gq_proj(h, mods, wqkv, qk_g, tables, tm=512):
    b, t, d = h.shape
    tm = min(tm, t)
    rope = tables is not None
    nq = GQ_HEADS * GQ_HEAD_DIM
    nk = GQ_KV_HEADS * GQ_HEAD_DIM
    tok = pl.BlockSpec((1, tm, d), lambda i, j: (i, j, 0))
    in_specs = [tok, _mod_spec(mods, d), _resident(wqkv.shape), _resident(qk_g.shape)]
    args = [h, mods, wqkv, qk_g]
    if rope:
        tab = pl.BlockSpec((tm, LANES), lambda i, j: (j, 0))
        in_specs += [tab, tab]
        args += list(tables)
    return pl.pallas_call(
        functools.partial(_gq_proj_kernel, rope=rope),
        grid=(b, t // tm),
        in_specs=in_specs,
        out_specs=[pl.BlockSpec((1, tm, nq), lambda i, j: (i, j, 0)),
                   pl.BlockSpec((1, tm, nk), lambda i, j: (i, j, 0)),
                   pl.BlockSpec((1, tm, nk), lambda i, j: (i, j, 0))],
        out_shape=[jax.ShapeDtypeStruct((b, t, nq), MXU_DTYPE),
                   jax.ShapeDtypeStruct((b, t, nk), MXU_DTYPE),
                   jax.ShapeDtypeStruct((b, t, nk), MXU_DTYPE)],
        compiler_params=_cparams(2),
        name="gq_proj",
    )(*args)


def _attn_pipeline(n_blocks, k_refs, s_scr, make_qs, finish):
    def scores(i):
        qs = make_qs(i)
        off = 0
        for k_ref in k_refs:
            n = k_ref.shape[1]
            s_scr[i % 2, :, off:off + n] = lax.dot_general(qs, k_ref[0], _NT, preferred_element_type=F32)
            off += n

    scores(0)
    for i in range(n_blocks):
        if i + 1 < n_blocks:
            scores(i + 1)
        s = s_scr[i % 2]
        p = jnp.exp2(s - jnp.max(s, axis=-1, keepdims=True))
        finish(i, p, jnp.sum(p, axis=-1, keepdims=True))


def _pv(a, v_refs):
    out, off = None, 0
    for v_ref in v_refs:
        n = v_ref.shape[1]
        part = _dot(a[:, off:off + n], v_ref[0])
        out = part if out is None else out + part
        off += n
    return out


def _da_attn_kernel(lam_ref, g_ref, q_ref, *rest, nseg, rb, lam_init):
    k_refs = [rest[2 * i] for i in range(nseg)]
    v_refs = [rest[2 * i + 1] for i in range(nseg)]
    o_ref, s_scr = rest[2 * nseg:]
    lv = lam_ref[...]
    lam = (jnp.exp(jnp.sum(lv[0:1] * lv[1:2], keepdims=True))
           - jnp.exp(jnp.sum(lv[2:3] * lv[3:4], keepdims=True)) + lam_init)
    gain = g_ref[...] * (1.0 - lam_init)
    lane = lax.broadcasted_iota(jnp.int32, (rb, LANES), 1)

    def make_qs(i):
        q = q_ref[0, i * rb:(i + 1) * rb, :]
        zero = jnp.zeros_like(q)
        return jnp.concatenate([jnp.where(lane < DA_HEAD_DIM, q, zero), jnp.where(lane >= DA_HEAD_DIM, q, zero)], axis=0)

    def finish(i, p, l):
        l1, l2 = l[:rb], l[rb:]
        ratio = lam * l1 / l2
        o = _pv(_mx(p[:rb] - ratio * p[rb:]), v_refs)
        y = _rms(o / l1) * gain
        o_ref[0, i * rb:(i + 1) * rb, :] = y.astype(o_ref.dtype)

    _attn_pipeline(q_ref.shape[1] // rb, k_refs, s_scr, make_qs, finish)


def _da_attn(q, kv_segs, lam_vecs, subln, lam_init, tq=512, rb=128):
    b, t, d = q.shape
    tq = min(tq, t)
    rb = min(rb, tq)
    in_specs = [_resident(lam_vecs.shape), _resident(subln.shape),
                pl.BlockSpec((1, tq, LANES), lambda i, h, j: (i, j, h))]
    args = [lam_vecs, subln, q]
    n_keys = 0
    for k, v in kv_segs:
        n = k.shape[1]
        n_keys += n
        seg = pl.BlockSpec((1, n, LANES), lambda i, h, j: (i, 0, h))
        in_specs += [seg, seg]
        args += [k, v]
    return pl.pallas_call(
        functools.partial(_da_attn_kernel, nseg=len(kv_segs), rb=rb, lam_init=lam_init),
        grid=(b, DA_HEADS, t // tq),
        in_specs=in_specs,
        out_specs=pl.BlockSpec((1, tq, LANES), lambda i, h, j: (i, j, h)),
        out_shape=jax.ShapeDtypeStruct((b, t, d), MXU_DTYPE),
        scratch_shapes=[pltpu.VMEM((2, 2 * rb, n_keys), F32)],
        compiler_params=_cparams(3),
        name="da_attn",
    )(*args)


def _gq_attn_kernel(q_ref, *rest, nseg, rb):
    k_refs = [rest[2 * i] for i in range(nseg)]
    v_refs = [rest[2 * i + 1] for i in range(nseg)]
    o_ref, s_scr = rest[2 * nseg:]

    def make_qs(i):
        return jnp.concatenate([q_ref[0, i * rb:(i + 1) * rb, m * LANES:(m + 1) * LANES] for m in range(GQ_GROUP)], axis=0)

    def finish(i, p, l):
        o = _pv(_mx(p), v_refs) / l
        for m in range(GQ_GROUP):
            o_ref[0, i * rb:(i + 1) * rb, m * LANES:(m + 1) * LANES] = o[m * rb:(m + 1) * rb].astype(o_ref.dtype)

    _attn_pipeline(q_ref.shape[1] // rb, k_refs, s_scr, make_qs, finish)


def _gq_attn(q, kv_segs, tq=256, rb=64):
    b, t, d = q.shape
    tq = min(tq, t)
    rb = min(rb, tq)
    gw = GQ_GROUP * GQ_HEAD_DIM
    in_specs = [pl.BlockSpec((1, tq, gw), lambda i, g, j: (i, j, g))]
    args = [q]
    n_keys = 0
    for k, v in kv_segs:
        n = k.shape[1]
        n_keys += n
        seg = pl.BlockSpec((1, n, LANES), lambda i, g, j: (i, 0, g))
        in_specs += [seg, seg]
        args += [k, v]
    return pl.pallas_call(
        functools.partial(_gq_attn_kernel, nseg=len(kv_segs), rb=rb),
        grid=(b, GQ_KV_HEADS, t // tq),
        in_specs=in_specs,
        out_specs=pl.BlockSpec((1, tq, gw), lambda i, g, j: (i, j, g)),
        out_shape=jax.ShapeDtypeStruct((b, t, d), MXU_DTYPE),
        scratch_shapes=[pltpu.VMEM((2, GQ_GROUP * rb, n_keys), F32)],
        compiler_params=_cparams(3),
        name="gq_attn",
    )(*args)


def _out_proj_kernel(h_ref, mod_ref, y_ref, w_ref, o_ref):
    gate = mod_ref[0, 5:6, :]
    o_ref[0] = h_ref[0] + gate * _dot(y_ref[0], w_ref[...])


def _out_proj(h, mods, y, wo, tm=512):
    b, t, d = h.shape
    tm = min(tm, t)
    tok = pl.BlockSpec((1, tm, d), lambda i, j: (i, j, 0))
    return pl.pallas_call(
        _out_proj_kernel,
        grid=(b, t // tm),
        in_specs=[tok, _mod_spec(mods, d), tok, _resident(wo.shape)],
        out_specs=tok,
        out_shape=jax.ShapeDtypeStruct((b, t, d), F32),
        compiler_params=_cparams(2),
        name="out_proj",
    )(h, mods, y, wo)


def _head_sum(x, e_ref):
    hi = _mx(x)
    lo = _mx(x - hi.astype(F32))
    return _dot(hi, e_ref[...]) + _dot(lo, e_ref[...])


def _rw_tok_kernel(h_ref, hp_ref, hn_ref, mod_ref, mu_ref, wr_ref, wk_ref, wv_ref,
                   w0_ref, w1_ref, w2_ref, a0_ref, a1_ref, a2_ref, g1_ref, g2_ref, kk_ref, ka_ref, e_ref,
                   r_out, v_out, lw_out, kd_out, kn_out, ba_out, g_out):
    t = pl.program_id(1)
    nt = pl.num_programs(1)
    tm = h_ref.shape[1]
    z = _modulate(h_ref[0], mod_ref, 3)
    zp = _modulate(hp_ref[0], mod_ref, 3)[SUBLANES - 1:SUBLANES]
    zn = _modulate(hn_ref[0], mod_ref, 3)[0:1]
    zp = jnp.where(t > 0, zp, jnp.zeros_like(zp))
    zn = jnp.where(t < nt - 1, zn, jnp.zeros_like(zn))
    row = lax.broadcasted_iota(jnp.int32, z.shape, 0)
    dp = jnp.where(row == 0, zp, pltpu.roll(z, 1, 0)) - z
    dn = jnp.where(row == tm - 1, zn, pltpu.roll(z, tm - 1, 0)) - z

    def mix(i):
        return z + dp * mu_ref[0, i:i + 1, :] + dn * mu_ref[1, i:i + 1, :]

    r = _dot(_mx(mix(0)), wr_ref[...])
    k = _dot(_mx(mix(2)), wk_ref[...])
    v = _dot(_mx(mix(3)), wv_ref[...])
    xw, xa, xg = _mx(mix(1)), _mx(mix(4)), _mx(mix(5))
    r_out[0] = r
    v_out[0] = v
    for d in range(2):
        x = -(w0_ref[d:d + 1, :] + _dot(_mx(jnp.tanh(_dot(xw, w1_ref[d]))), w2_ref[d]))
        softplus = jnp.maximum(x, 0.0) + jnp.log(1.0 + jnp.exp(-jnp.abs(x)))
        lw_out[d, 0] = -jnp.exp(-softplus - 0.5)
        a = jax.nn.sigmoid(a0_ref[d:d + 1, :] + _dot(_mx(_dot(xa, a1_ref[d])), a2_ref[d]))
        g_out[d, 0] = _dot(_mx(jax.nn.sigmoid(_dot(xg, g1_ref[d]))), g2_ref[d])
        kx = k * kk_ref[d:d + 1, :]
        kn = kx / jnp.maximum(jnp.sqrt(_head_sum(kx * kx, e_ref)), 1e-12)
        kn_out[d, 0] = kn
        ba_out[d, 0] = kn * a
        kd_out[d, 0] = k * (1.0 + (a - 1.0) * ka_ref[d:d + 1, :])


def _rw_tok(h, mods, p, tm=128):
    b, t, d = h.shape
    tm = min(tm, t)
    nb8 = t // SUBLANES
    per8 = tm // SUBLANES
    tok = pl.BlockSpec((1, tm, d), lambda i, j: (i, j, 0))
    prev = pl.BlockSpec((1, SUBLANES, d), lambda i, j: (i, jnp.maximum(j * per8 - 1, 0), 0))
    nxt = pl.BlockSpec((1, SUBLANES, d), lambda i, j: (i, jnp.minimum((j + 1) * per8, nb8 - 1), 0))
    names = ("mu", "wr", "wk", "wv", "w0", "w1", "w2", "a0", "a1", "a2", "g1", "g2", "kk", "ka", "e")
    weights = [p[n] for n in names]
    both = pl.BlockSpec((2, 1, tm, d), lambda i, j: (0, i, j, 0))
    one = jax.ShapeDtypeStruct((b, t, d), F32)
    two = jax.ShapeDtypeStruct((2, b, t, d), F32)
    return pl.pallas_call(
        _rw_tok_kernel,
        grid=(b, t // tm),
        in_specs=[tok, prev, nxt, _mod_spec(mods, d)] + [_resident(w.shape) for w in weights],
        out_specs=[tok, tok, both, both, both, both, both],
        out_shape=[one, one, two, two, two, two, two],
        compiler_params=_cparams(2),
        name="rw_tok",
    )(h, h, h, mods, *weights)


def _mm(a, b, dims=_NN):
    return lax.dot_general(_mx(a), _mx(b), dims, preferred_element_type=F32)


def _split(x):
    hi = _mx(x)
    return hi, _mx(x - hi.astype(F32))


def _mm3(a, b, dims=_NN):
    ah, al = _split(a)
    bh, bl = _split(b)
    dg = functools.partial(lax.dot_general, dimension_numbers=dims, preferred_element_type=F32)
    return dg(ah, bh) + (dg(al, bh) + dg(ah, bl))


def _cumsum_mm(tri, x):
    h1, rest = _mx(x), x - _mx(x).astype(F32)
    h2, h3 = _split(rest)
    dg = functools.partial(lax.dot_general, dimension_numbers=_NN, preferred_element_type=F32)
    return dg(tri, h1) + (dg(tri, h2) + dg(tri, h3))


def _scan_kernel(s0_ref, lw_ref, kn_ref, ba_ref, kd_ref, r_ref, v_ref, y_ref, sf_ref, s_scr, *, rev):
    L = SCAN_CHUNK
    t = pl.program_id(1)
    n_pairs = s_scr.shape[0]
    n_chunks = lw_ref.shape[2] // L

    @pl.when(t == 0)
    def _():
        s_scr[...] = s0_ref[0]

    r1 = lax.broadcasted_iota(jnp.int32, (L, LANES), 0)
    c1 = lax.broadcasted_iota(jnp.int32, (L, LANES), 1) & (L - 1)
    incl = (c1 >= r1) if rev else (c1 <= r1)
    strict = (c1 > r1) if rev else (c1 < r1)
    tr = lax.broadcasted_iota(jnp.int32, (L, L), 0)
    tc = lax.broadcasted_iota(jnp.int32, (L, L), 1)
    tri = jnp.where((tc >= tr) if rev else (tc <= tr), 1.0, 0.0).astype(MXU_DTYPE)
    r2 = lax.broadcasted_iota(jnp.int32, (2 * L, LANES), 0)
    c2 = lax.broadcasted_iota(jnp.int32, (2 * L, LANES), 1)
    own = (r2 < L) == (c2 < L)
    eye = jnp.where(r2 == c2, 1.0, 0.0).astype(F32)
    rt6, ct6 = r2 & (L - 1), c2 & (L - 1)
    late, early = (ct6, rt6) if rev else (rt6, ct6)
    off_masks = []
    for i in range(int(math.log2(L))):
        s = 1 << i
        off_masks.append(((rt6 >> (i + 1)) == (ct6 >> (i + 1))) & ((late & s) != 0) & ((early & s) == 0))

    def bd(x):
        return jnp.where(own, jnp.concatenate([x, x], axis=0), 0.0)

    def chunk(ci, carry):
        cc = (n_chunks - 1 - ci) if rev else ci
        rows = pl.ds(pl.multiple_of(cc * L, L), L)
        pairs = range(n_pairs)
        lanes = [slice(p * LANES, (p + 1) * LANES) for p in pairs]

        def each(fn, *lists):
            return [fn(*args) for args in zip(*lists)]

        lw = [lw_ref[0, 0, rows, ln] for ln in lanes]
        cin = each(lambda x: _cumsum_mm(tri, x), lw)
        e_in = each(jnp.exp, cin)
        e_neg = each(lambda c: jnp.exp(-c), cin)
        at = each(lambda ln, c, x: -kn_ref[0, 0, rows, ln] * jnp.exp(c - x), lanes, cin, lw)
        rt = each(lambda ln, e: r_ref[0, rows, ln] * e, lanes, e_in)
        bt = each(lambda ln, e: ba_ref[0, 0, rows, ln] * e, lanes, e_neg)
        kt = each(lambda ln, e: kd_ref[0, 0, rows, ln] * e, lanes, e_neg)
        wl = each(lambda e: e[0:1] if rev else e[L - 1:L], e_in)
        g = each(lambda a, r_, b_, k_: _mm(jnp.concatenate([a, r_], axis=0),
                                           jnp.concatenate([bd(b_), bd(k_)], axis=0), _NT), at, rt, bt, kt)
        n_bd = each(lambda x: bd(jnp.where(strict, x[:L, :LANES], 0.0)), g)
        arb = each(lambda x: bd(jnp.where(incl, x[L:, :LANES], 0.0)), g)
        aak = each(lambda x: bd(jnp.where(strict, x[:L, LANES:], 0.0)), g)
        ark = each(lambda x: bd(jnp.where(incl, x[L:, LANES:], 0.0)), g)
        tinv = each(lambda n: eye + jnp.where(off_masks[0], n, 0.0), n_bd)
        for off in off_masks[1:]:
            inner = each(lambda n, t_: _mm(jnp.where(off, n, 0.0), t_), n_bd, tinv)
            tinv = each(lambda t_, x: t_ + _mm(t_, x), tinv, inner)
        v_sd = each(lambda ln: bd(v_ref[0, rows, ln]), lanes)
        akv = each(_mm, aak, v_sd)
        pq = each(lambda t_, a, x: _mm(t_, jnp.concatenate([bd(a), x], axis=1)), tinv, at, akv)
        bh = each(lambda b_, w: bd(b_ * w), bt, wl)
        kh = each(lambda k_, w: bd(k_ * w), kt, wl)
        qv = each(lambda x, v_: jnp.concatenate([x[:, LANES:], v_], axis=0), pq, v_sd)
        ptb = each(lambda x, b_: _mm(x[:, :LANES].T, b_), pq, bh)
        c_mat = each(lambda x, b_, k_: _mm(x.T, jnp.concatenate([b_, k_], axis=0)), qv, bh, kh)
        ry = each(lambda r_, a, x: bd(r_) + _mm(a, x[:, :LANES]), rt, arb, pq)
        y0 = each(lambda a, k_, x: _mm(jnp.concatenate([a, k_], axis=1), x), arb, ark, qv)
        s_in = [s_scr[p] for p in pairs]
        ysd = each(lambda r_, s_, y_: _mm3(r_, s_, _NT) + y_, ry, s_in, y0)
        for ln, y_ in zip(lanes, ysd):
            y_ref[0, rows, ln] = y_[:L] + y_[L:]
        s_out = each(lambda s_, w, x, c: s_ * w + _mm3(s_, x) + c, s_in, wl, ptb, c_mat)
        for p, s_ in zip(pairs, s_out):
            s_scr[p] = s_
        return carry

    lax.fori_loop(0, n_chunks, chunk, 0)
    sf_ref[0] = s_scr[...]


def _scan(s0, lw, kn, ba, kd, r, v, direction, tb=256):
    b, t, d = r.shape
    tb = min(tb, t)
    nb = t // tb
    rev = direction == 1
    n_pairs = d // LANES

    def tmap(j):
        return (nb - 1 - j) if rev else j

    dirspec = pl.BlockSpec((1, 1, tb, d), lambda i, j: (direction, i, tmap(j), 0))
    tok = pl.BlockSpec((1, tb, d), lambda i, j: (i, tmap(j), 0))
    st = pl.BlockSpec((1, n_pairs, LANES, LANES), lambda i, j: (i, 0, 0, 0))
    return pl.pallas_call(
        functools.partial(_scan_kernel, rev=rev),
        grid=(b, nb),
        in_specs=[st, dirspec, dirspec, dirspec, dirspec, tok, tok],
        out_specs=[tok, st],
        out_shape=[jax.ShapeDtypeStruct((b, t, d), F32),
                   jax.ShapeDtypeStruct((b, n_pairs, LANES, LANES), F32)],
        scratch_shapes=[pltpu.VMEM((n_pairs, LANES, LANES), F32)],
        compiler_params=_cparams(2),
        name="rw_scan_rev" if rev else "rw_scan_fwd",
    )(s0, lw, kn, ba, kd, r, v)


def _rw_out_kernel(h_ref, mod_ref, y0_ref, y1_ref, kd_ref, g_ref, r_ref, v_ref,
                   rk_ref, lng_ref, lnb_ref, e_ref, wo_ref, o_ref):
    r = r_ref[0]
    v = v_ref[0]
    inv_n = 1.0 / RW_HEAD
    total = jnp.zeros(r.shape, F32)
    for d, y_ref in enumerate((y0_ref, y1_ref)):
        y = y_ref[0]
        yc = y - _head_sum(y, e_ref) * inv_n
        var = _head_sum(yc * yc, e_ref) * inv_n
        yn = yc * lax.rsqrt(var + RW_GN_EPS)
        bonus = _head_sum(r * kd_ref[d, 0] * rk_ref[d:d + 1, :], e_ref) * v
        out = yn * lng_ref[d:d + 1, :] + lnb_ref[d:d + 1, :] + bonus
        total = total + out * g_ref[d, 0]
    gate = mod_ref[0, 5:6, :]
    o_ref[0] = h_ref[0] + gate * _dot(_mx(total), wo_ref[...])


def _rw_out(h, mods, y0, y1, kd, g, r, v, p, tm=256):
    b, t, d = h.shape
    tm = min(tm, t)
    tok = pl.BlockSpec((1, tm, d), lambda i, j: (i, j, 0))
    both = pl.BlockSpec((2, 1, tm, d), lambda i, j: (0, i, j, 0))
    weights = [p["rk"], p["ln_g"], p["ln_b"], p["e"], p["wo"]]
    return pl.pallas_call(
        _rw_out_kernel,
        grid=(b, t // tm),
        in_specs=[tok, _mod_spec(mods, d), tok, tok, both, both, tok, tok] + [_resident(w.shape) for w in weights],
        out_specs=tok,
        out_shape=jax.ShapeDtypeStruct((b, t, d), F32),
        compiler_params=_cparams(2),
        name="rw_out",
    )(h, mods, y0, y1, kd, g, r, v, *weights)


def _final_norm_kernel(h_ref, g_ref, o_ref):
    o_ref[0] = _rms(h_ref[0]) * g_ref[...]


def _final_norm(h, gain, tm=512):
    b, t, d = h.shape
    tm = min(tm, t)
    tok = pl.BlockSpec((1, tm, d), lambda i, j: (i, j, 0))
    return pl.pallas_call(
        _final_norm_kernel,
        grid=(b, t // tm),
        in_specs=[tok, _resident((1, d))],
        out_specs=tok,
        out_shape=jax.ShapeDtypeStruct((b, t, d), F32),
        compiler_params=_cparams(2),
        name="final_norm",
    )(h, gain.reshape(1, d))


def _da_layer(h, hc, mod, modc, wqkv, wo, lam_vecs, subln, lam_init, need_ctx):
    s = h.shape[1]
    wqkv, wo = _mx(wqkv), _mx(wo)
    subln = subln.reshape(1, -1)
    q, k, v = _da_proj(h, mod, wqkv, _rope_tables(s, DA_HEAD_DIM))
    qc, kc, vc = _da_proj(hc, modc, wqkv, None)
    y = _da_attn(q, [(kc, vc), (k, v)], lam_vecs, subln, lam_init)
    h = _out_proj(h, mod, y, wo)
    if need_ctx:
        yc = _da_attn(qc, [(kc, vc)], lam_vecs, subln, lam_init)
        hc = _out_proj(hc, modc, yc, wo)
    return h, hc


def _gq_layer(h, hc, mod, modc, wqkv, wo, qk_g, need_ctx):
    s = h.shape[1]
    wqkv, wo = _mx(wqkv), _mx(wo)
    q, k, v = _gq_proj(h, mod, wqkv, qk_g, _rope_tables(s, GQ_HEAD_DIM))
    qc, kc, vc = _gq_proj(hc, modc, wqkv, qk_g, None)
    y = _gq_attn(q, [(kc, vc), (k, v)])
    h = _out_proj(h, mod, y, wo)
    if need_ctx:
        yc = _gq_attn(qc, [(kc, vc)])
        hc = _out_proj(hc, modc, yc, wo)
    return h, hc


def _rw_layer(h, hc, mod, modc, p, need_ctx):
    b, _, d = h.shape
    group = lax.broadcasted_iota(jnp.int32, (d, d), 0) // RW_HEAD == lax.broadcasted_iota(jnp.int32, (d, d), 1) // RW_HEAD
    p = dict(p, e=group.astype(MXU_DTYPE))
    r, v, lw, kd, kn, ba, g = _rw_tok(h, mod, p)
    rc, vc, lwc, kdc, knc, bac, gc = _rw_tok(hc, modc, p)
    zero = jnp.zeros((b, d // LANES, LANES, LANES), F32)
    ys, ycs = [], []
    for direction in range(2):
        yc, state_c = _scan(zero, lwc, knc, bac, kdc, rc, vc, direction)
        yl, _ = _scan(state_c, lw, kn, ba, kd, r, v, direction)
        ys.append(yl)
        ycs.append(yc)
    h = _rw_out(h, mod, ys[0], ys[1], kd, g, r, v, p)
    if need_ctx:
        hc = _rw_out(hc, modc, ycs[0], ycs[1], kdc, gc, rc, vc, p)
    return h, hc


def kernel(x, c, ctx, c_ctx, ada_w, ada_b, ffn_wg, ffn_wu, ffn_wd, final_g, da_wqkv, da_wo, da_lam, da_subln, gq_wqkv, gq_wo, gq_qk_g, rw_mu, rw_wr, rw_wk, rw_wv, rw_wo, rw_w0, rw_w1, rw_w2, rw_a0, rw_a1, rw_a2, rw_g1, rw_g2, rw_kk, rw_ka, rw_rk, rw_ln_g, rw_ln_b):
    b, _, d = x.shape
    depth = ada_w.shape[0]
    rows = -(-(b + 1) // SUBLANES) * SUBLANES
    cvec = jnp.zeros((rows, d), F32).at[:b].set(c).at[b].set(c_ctx)
    mods = _adaln(cvec, ada_w, ada_b)
    h, hc = x, ctx
    for i in range(depth):
        kind, j = i % 3, i // 3
        need_ctx = i < depth - 1
        mod = mods[i, :b].reshape(b, N_MOD, d)
        modc = mods[i, b:b + 1].reshape(1, N_MOD, d)
        wg, wu, wd = _mx(ffn_wg[i, 0]), _mx(ffn_wu[i, 0]), _mx(ffn_wd[i, 0])
        h = _ffn(h, mod, wg, wu, wd, 0)
        hc = _ffn(hc, modc, wg, wu, wd, 0)
        if kind == 0:
            lam_init = 0.8 - 0.6 * math.exp(-0.3 * i)
            h, hc = _da_layer(h, hc, mod, modc, da_wqkv[j], da_wo[j], da_lam[j], da_subln[j], lam_init, need_ctx)
        elif kind == 1:
            h, hc = _gq_layer(h, hc, mod, modc, gq_wqkv[j], gq_wo[j], gq_qk_g[j], need_ctx)
        else:
            p = dict(mu=rw_mu[j], wr=_mx(rw_wr[j]), wk=_mx(rw_wk[j]), wv=_mx(rw_wv[j]), wo=_mx(rw_wo[j]),
                     w0=rw_w0[j], w1=_mx(rw_w1[j]), w2=_mx(rw_w2[j]), a0=rw_a0[j], a1=_mx(rw_a1[j]),
                     a2=_mx(rw_a2[j]), g1=_mx(rw_g1[j]), g2=_mx(rw_g2[j]), kk=rw_kk[j], ka=rw_ka[j],
                     rk=rw_rk[j].reshape(2, d), ln_g=rw_ln_g[j], ln_b=rw_ln_b[j])
            h, hc = _rw_layer(h, hc, mod, modc, p, need_ctx)
        wg, wu, wd = _mx(ffn_wg[i, 1]), _mx(ffn_wu[i, 1]), _mx(ffn_wd[i, 1])
        h = _ffn(h, mod, wg, wu, wd, 6)
        if need_ctx:
            hc = _ffn(hc, modc, wg, wu, wd, 6)
    return _final_norm(h, final_g)
```

```python
import functools
import math

import jax
import jax.numpy as jnp
from jax import lax
from jax.experimental import pallas as pl
from jax.experimental.pallas import tpu as pltpu

F32 = jnp.float32
MXU_DTYPE = jnp.bfloat16

NORM_EPS = 1e-6
ROPE_THETA = 10000.0
GRID_W = 64
N_MOD = 9
DA_HEADS = 8
DA_HEAD_DIM = 64
GQ_HEADS = 8
GQ_KV_HEADS = 2
GQ_HEAD_DIM = 128
GQ_GROUP = GQ_HEADS // GQ_KV_HEADS
RW_HEAD = 64
RW_GN_EPS = 64e-5

LANES = 128
SUBLANES = 8
VMEM_LIMIT_BYTES = 56 * 1024 * 1024

SCAN_CHUNK = 64


def _cparams(n_axes):
    return pltpu.CompilerParams(
        dimension_semantics=("arbitrary",) * n_axes,
        vmem_limit_bytes=VMEM_LIMIT_BYTES,
    )


def _resident(shape):
    zeros = (0,) * len(shape)
    return pl.BlockSpec(shape, lambda *_: zeros)


def _mod_spec(mods, d):
    if mods.shape[0] > 1:
        return pl.BlockSpec((1, N_MOD, d), lambda b, t: (b, 0, 0))
    return pl.BlockSpec((1, N_MOD, d), lambda b, t: (0, 0, 0))


def _rms(x):
    return x * lax.rsqrt(jnp.mean(x * x, axis=-1, keepdims=True) + NORM_EPS)


def _modulate(x, mod_ref, base):
    shift = mod_ref[0, base:base + 1, :]
    scale = mod_ref[0, base + 1:base + 2, :]
    return _rms(x) * (1.0 + scale) + shift


def _dot(a, b):
    return jnp.dot(a, b, preferred_element_type=F32)


def _mx(x):
    return x.astype(MXU_DTYPE)


_NT = (((1,), (1,)), ((), ()))
_TN = (((0,), (0,)), ((), ()))
_NN = (((1,), (0,)), ((), ()))
LOG2E = math.log2(math.e)


def _adaln_kernel(c_ref, w_ref, b_ref, o_ref):
    c = c_ref[...]
    act = _mx(c * jax.nn.sigmoid(c))
    o_ref[0] = _dot(act, _mx(w_ref[0])) + b_ref[0]


def _adaln(cvec, ada_w, ada_b):
    depth, d, n = ada_w.shape
    rows = cvec.shape[0]
    tn = 1024
    return pl.pallas_call(
        _adaln_kernel,
        grid=(depth, n // tn),
        in_specs=[
            pl.BlockSpec((rows, d), lambda i, j: (0, 0)),
            pl.BlockSpec((1, d, tn), lambda i, j: (i, 0, j)),
            pl.BlockSpec((1, 1, tn), lambda i, j: (i, 0, j)),
        ],
        out_specs=pl.BlockSpec((1, rows, tn), lambda i, j: (i, 0, j)),
        out_shape=jax.ShapeDtypeStruct((depth, rows, n), F32),
        compiler_params=_cparams(2),
        name="adaln",
    )(cvec, ada_w, ada_b.reshape(depth, 1, n))


def _ffn_kernel(h_ref, mod_ref, wg_ref, wu_ref, wd_ref, *rest, base, tf):
    o_ref = rest[-1]
    x = h_ref[0]
    xn = _mx(_modulate(x, mod_ref, base))
    f = wg_ref.shape[1]
    acc = jnp.zeros(x.shape, F32)
    for c in range(f // tf):
        g = _dot(xn, wg_ref[:, c * tf:(c + 1) * tf])
        u = _dot(xn, wu_ref[:, c * tf:(c + 1) * tf])
        a = _mx(g * jax.nn.sigmoid(g) * u)
        acc = acc + _dot(a, wd_ref[c * tf:(c + 1) * tf, :])
    gate = mod_ref[0, base + 2:base + 3, :]
    out = x + (0.5 * gate) * acc
    if len(rest) == 2:
        out = _rms(out) * rest[0][...]
    o_ref[0] = out


def _ffn(h, mods, wg, wu, wd, base, final_gain=None, tm=512, tf=256):
    b, t, d = h.shape
    f = wg.shape[1]
    tm = min(tm, t)
    in_specs = [
        pl.BlockSpec((1, tm, d), lambda i, j: (i, j, 0)),
        _mod_spec(mods, d),
        _resident((d, f)), _resident((d, f)), _resident((f, d)),
    ]
    args = [h, mods, wg, wu, wd]
    if final_gain is not None:
        in_specs.append(_resident((1, d)))
        args.append(final_gain.reshape(1, d))
    return pl.pallas_call(
        functools.partial(_ffn_kernel, base=base, tf=tf),
        grid=(b, t // tm),
        in_specs=in_specs,
        out_specs=pl.BlockSpec((1, tm, d), lambda i, j: (i, j, 0)),
        out_shape=jax.ShapeDtypeStruct((b, t, d), F32),
        compiler_params=_cparams(2),
        name="ffn",
    )(*args)


def _rope_tables(n_tokens, head_dim):
    rows = n_tokens // GRID_W
    row = jnp.repeat(jnp.arange(rows, dtype=jnp.int32), GRID_W).astype(F32)
    col = jnp.tile(jnp.arange(GRID_W, dtype=jnp.int32), rows).astype(F32)
    n_freq = head_dim // 4
    inv = jnp.power(ROPE_THETA, -jnp.arange(n_freq, dtype=F32) / n_freq)
    ang = jnp.concatenate([row[:, None] * inv, col[:, None] * inv], axis=-1)
    cos, sin = jnp.cos(ang), jnp.sin(ang)
    reps = LANES // head_dim
    cos_t = jnp.tile(jnp.concatenate([cos, cos], axis=-1), (1, reps))
    sin_t = jnp.tile(jnp.concatenate([-sin, sin], axis=-1), (1, reps))
    return cos_t, sin_t


def _rope_slab(x, cos, sin, half):
    if 2 * half == LANES:
        swapped = pltpu.roll(x, half, 1)
    else:
        lane = lax.broadcasted_iota(jnp.int32, x.shape, 1)
        first = (lane & (2 * half - 1)) < half
        swapped = jnp.where(first, pltpu.roll(x, LANES - half, 1), pltpu.roll(x, half, 1))
    return x * cos + swapped * sin


def _da_proj_kernel(h_ref, mod_ref, w_ref, *rest, rope):
    if rope:
        cos_ref, sin_ref, q_ref, k_ref, v_ref = rest
        cos, sin = cos_ref[...], sin_ref[...]
    else:
        q_ref, k_ref, v_ref = rest
    d = h_ref.shape[2]
    u = _mx(_modulate(h_ref[0], mod_ref, 3))
    scale = DA_HEAD_DIM ** -0.5 * LOG2E
    for idx, (o_ref, rot, mul) in enumerate(((q_ref, True, scale), (k_ref, True, 1.0), (v_ref, False, 1.0))):
        y = _dot(u, w_ref[:, idx * d:(idx + 1) * d])
        for j in range(d // LANES):
            ys = y[:, j * LANES:(j + 1) * LANES]
            if rot and rope:
                ys = _rope_slab(ys, cos, sin, DA_HEAD_DIM // 2)
            if mul != 1.0:
                ys = ys * mul
            o_ref[0, :, j * LANES:(j + 1) * LANES] = ys.astype(o_ref.dtype)


def _da_proj(h, mods, wqkv, tables, tm=512):
    b, t, d = h.shape
    tm = min(tm, t)
    rope = tables is not None
    tok = pl.BlockSpec((1, tm, d), lambda i, j: (i, j, 0))
    in_specs = [tok, _mod_spec(mods, d), _resident(wqkv.shape)]
    args = [h, mods, wqkv]
    if rope:
        tab = pl.BlockSpec((tm, LANES), lambda i, j: (j, 0))
        in_specs += [tab, tab]
        args += list(tables)
    out = jax.ShapeDtypeStruct((b, t, d), MXU_DTYPE)
    return pl.pallas_call(
        functools.partial(_da_proj_kernel, rope=rope),
        grid=(b, t // tm),
        in_specs=in_specs,
        out_specs=[tok, tok, tok],
        out_shape=[out, out, out],
        compiler_params=_cparams(2),
        name="da_proj",
    )(*args)


def _gq_proj_kernel(h_ref, mod_ref, w_ref, g_ref, *rest, rope):
    if rope:
        cos_ref, sin_ref, q_ref, k_ref, v_ref = rest
        cos, sin = cos_ref[...], sin_ref[...]
    else:
        q_ref, k_ref, v_ref = rest
    u = _mx(_modulate(h_ref[0], mod_ref, 3))
    nq = GQ_HEADS * GQ_HEAD_DIM
    nk = GQ_KV_HEADS * GQ_HEAD_DIM
    scale = GQ_HEAD_DIM ** -0.5 * LOG2E
    plan = ((q_ref, 0, nq, 0, scale), (k_ref, nq, nk, 1, 1.0), (v_ref, nq + nk, nk, None, 1.0))
    for o_ref, start, width, gain_row, mul in plan:
        y = _dot(u, w_ref[:, start:start + width])
        for j in range(width // LANES):
            ys = y[:, j * LANES:(j + 1) * LANES]
            if gain_row is not None:
                ys = _rms(ys) * g_ref[gain_row:gain_row + 1, :]
                if rope:
                    ys = _rope_slab(ys, cos, sin, GQ_HEAD_DIM // 2)
            if mul != 1.0:
                ys = ys * mul
            o_ref[0, :, j * LANES:(j + 1) * LANES] = ys.astype(o_ref.dtype)


def _gq_proj(h, mods, wqkv, qk_g, tables, tm=512):
    b, t, d = h.shape
    tm = min(tm, t)
    rope = tables is not None
    nq = GQ_HEADS * GQ_HEAD_DIM
    nk = GQ_KV_HEADS * GQ_HEAD_DIM
    tok = pl.BlockSpec((1, tm, d), lambda i, j: (i, j, 0))
    in_specs = [tok, _mod_spec(mods, d), _resident(wqkv.shape), _resident(qk_g.shape)]
    args = [h, mods, wqkv, qk_g]
    if rope:
        tab = pl.BlockSpec((tm, LANES), lambda i, j: (j, 0))
        in_specs += [tab, tab]
        args += list(tables)
    return pl.pallas_call(
        functools.partial(_gq_proj_kernel, rope=rope),
        grid=(b, t // tm),
        in_specs=in_specs,
        out_specs=[pl.BlockSpec((1, tm, nq), lambda i, j: (i, j, 0)),
                   pl.BlockSpec((1, tm, nk), lambda i, j: (i, j, 0)),
                   pl.BlockSpec((1, tm, nk), lambda i, j: (i, j, 0))],
        out_shape=[jax.ShapeDtypeStruct((b, t, nq), MXU_DTYPE),
                   jax.ShapeDtypeStruct((b, t, nk), MXU_DTYPE),
                   jax.ShapeDtypeStruct((b, t, nk), MXU_DTYPE)],
        compiler_params=_cparams(2),
        name="gq_proj",
    )(*args)


def _attn_pipeline(n_blocks, k_refs, s_scr, make_qs, finish):
    def scores(i):
        qs = make_qs(i)
        off = 0
        for k_ref in k_refs:
            n = k_ref.shape[1]
            s_scr[i % 2, :, off:off + n] = lax.dot_general(qs, k_ref[0], _NT, preferred_element_type=F32)
            off += n

    scores(0)
    for i in range(n_blocks):
        if i + 1 < n_blocks:
            scores(i + 1)
        s = s_scr[i % 2]
        p = jnp.exp2(s - jnp.max(s, axis=-1, keepdims=True))
        finish(i, p, jnp.sum(p, axis=-1, keepdims=True))


def _pv(a, v_refs):
    out, off = None, 0
    for v_ref in v_refs:
        n = v_ref.shape[1]
        part = _dot(a[:, off:off + n], v_ref[0])
        out = part if out is None else out + part
        off += n
    return out


def _da_attn_kernel(lam_ref, g_ref, q_ref, *rest, nseg, rb, lam_init):
    k_refs = [rest[2 * i] for i in range(nseg)]
    v_refs = [rest[2 * i + 1] for i in range(nseg)]
    o_ref, s_scr = rest[2 * nseg:]
    lv = lam_ref[...]
    lam = (jnp.exp(jnp.sum(lv[0:1] * lv[1:2], keepdims=True))
           - jnp.exp(jnp.sum(lv[2:3] * lv[3:4], keepdims=True)) + lam_init)
    gain = g_ref[...] * (1.0 - lam_init)
    lane = lax.broadcasted_iota(jnp.int32, (rb, LANES), 1)

    def make_qs(i):
        q = q_ref[0, i * rb:(i + 1) * rb, :]
        zero = jnp.zeros_like(q)
        return jnp.concatenate([jnp.where(lane < DA_HEAD_DIM, q, zero), jnp.where(lane >= DA_HEAD_DIM, q, zero)], axis=0)

    def finish(i, p, l):
        l1, l2 = l[:rb], l[rb:]
        ratio = lam * l1 / l2
        o = _pv(_mx(p[:rb] - ratio * p[rb:]), v_refs)
        y = _rms(o / l1) * gain
        o_ref[0, i * rb:(i + 1) * rb, :] = y.astype(o_ref.dtype)

    _attn_pipeline(q_ref.shape[1] // rb, k_refs, s_scr, make_qs, finish)


def _da_attn(q, kv_segs, lam_vecs, subln, lam_init, tq=512, rb=128):
    b, t, d = q.shape
    tq = min(tq, t)
    rb = min(rb, tq)
    in_specs = [_resident(lam_vecs.shape), _resident(subln.shape),
                pl.BlockSpec((1, tq, LANES), lambda i, h, j: (i, j, h))]
    args = [lam_vecs, subln, q]
    n_keys = 0
    for k, v in kv_segs:
        n = k.shape[1]
        n_keys += n
        seg = pl.BlockSpec((1, n, LANES), lambda i, h, j: (i, 0, h))
        in_specs += [seg, seg]
        args += [k, v]
    return pl.pallas_call(
        functools.partial(_da_attn_kernel, nseg=len(kv_segs), rb=rb, lam_init=lam_init),
        grid=(b, DA_HEADS, t // tq),
        in_specs=in_specs,
        out_specs=pl.BlockSpec((1, tq, LANES), lambda i, h, j: (i, j, h)),
        out_shape=jax.ShapeDtypeStruct((b, t, d), MXU_DTYPE),
        scratch_shapes=[pltpu.VMEM((2, 2 * rb, n_keys), F32)],
        compiler_params=_cparams(3),
        name="da_attn",
    )(*args)


def _gq_attn_kernel(q_ref, *rest, nseg, rb):
    k_refs = [rest[2 * i] for i in range(nseg)]
    v_refs = [rest[2 * i + 1] for i in range(nseg)]
    o_ref, s_scr = rest[2 * nseg:]

    def make_qs(i):
        return jnp.concatenate([q_ref[0, i * rb:(i + 1) * rb, m * LANES:(m + 1) * LANES] for m in range(GQ_GROUP)], axis=0)

    def finish(i, p, l):
        o = _pv(_mx(p), v_refs) / l
        for m in range(GQ_GROUP):
            o_ref[0, i * rb:(i + 1) * rb, m * LANES:(m + 1) * LANES] = o[m * rb:(m + 1) * rb].astype(o_ref.dtype)

    _attn_pipeline(q_ref.shape[1] // rb, k_refs, s_scr, make_qs, finish)


def _gq_attn(q, kv_segs, tq=256, rb=32):
    b, t, d = q.shape
    tq = min(tq, t)
    rb = min(rb, tq)
    gw = GQ_GROUP * GQ_HEAD_DIM
    in_specs = [pl.BlockSpec((1, tq, gw), lambda i, g, j: (i, j, g))]
    args = [q]
    n_keys = 0
    for k, v in kv_segs:
        n = k.shape[1]
        n_keys += n
        seg = pl.BlockSpec((1, n, LANES), lambda i, g, j: (i, 0, g))
        in_specs += [seg, seg]
        args += [k, v]
    return pl.pallas_call(
        functools.partial(_gq_attn_kernel, nseg=len(kv_segs), rb=rb),
        grid=(b, GQ_KV_HEADS, t // tq),
        in_specs=in_specs,
        out_specs=pl.BlockSpec((1, tq, gw), lambda i, g, j: (i, j, g)),
        out_shape=jax.ShapeDtypeStruct((b, t, d), MXU_DTYPE),
        scratch_shapes=[pltpu.VMEM((2, GQ_GROUP * rb, n_keys), F32)],
        compiler_params=_cparams(3),
        name="gq_attn",
    )(*args)


def _out_proj_kernel(h_ref, mod_ref, y_ref, w_ref, o_ref):
    gate = mod_ref[0, 5:6, :]
    o_ref[0] = h_ref[0] + gate * _dot(y_ref[0], w_ref[...])


def _out_proj(h, mods, y, wo, tm=512):
    b, t, d = h.shape
    tm = min(tm, t)
    tok = pl.BlockSpec((1, tm, d), lambda i, j: (i, j, 0))
    return pl.pallas_call(
        _out_proj_kernel,
        grid=(b, t // tm),
        in_specs=[tok, _mod_spec(mods, d), tok, _resident(wo.shape)],
        out_specs=tok,
        out_shape=jax.ShapeDtypeStruct((b, t, d), F32),
        compiler_params=_cparams(2),
        name="out_proj",
    )(h, mods, y, wo)


def _head_sum(x, e_ref):
    return _dot(_mx(x), e_ref[...])


def _rw_tok_kernel(h_ref, hp_ref, hn_ref, mod_ref, mu_ref, wr_ref, wk_ref, wv_ref,
                   w0_ref, w1_ref, w2_ref, a0_ref, a1_ref, a2_ref, g1_ref, g2_ref, kk_ref, ka_ref, e_ref,
                   r_out, v_out, lw_out, kd_out, kn_out, ba_out, g_out):
    t = pl.program_id(1)
    nt = pl.num_programs(1)
    tm = h_ref.shape[1]
    z = _modulate(h_ref[0], mod_ref, 3)
    zp = _modulate(hp_ref[0], mod_ref, 3)[SUBLANES - 1:SUBLANES]
    zn = _modulate(hn_ref[0], mod_ref, 3)[0:1]
    zp = jnp.where(t > 0, zp, jnp.zeros_like(zp))
    zn = jnp.where(t < nt - 1, zn, jnp.zeros_like(zn))
    row = lax.broadcasted_iota(jnp.int32, z.shape, 0)
    dp = jnp.where(row == 0, zp, pltpu.roll(z, 1, 0)) - z
    dn = jnp.where(row == tm - 1, zn, pltpu.roll(z, tm - 1, 0)) - z

    def mix(i):
        return z + dp * mu_ref[0, i:i + 1, :] + dn * mu_ref[1, i:i + 1, :]

    r = _dot(_mx(mix(0)), wr_ref[...])
    k = _dot(_mx(mix(2)), wk_ref[...])
    v = _dot(_mx(mix(3)), wv_ref[...])
    xw, xa, xg = _mx(mix(1)), _mx(mix(4)), _mx(mix(5))
    r_out[0] = r
    v_out[0] = v
    for d in range(2):
        x = -(w0_ref[d:d + 1, :] + _dot(_mx(jnp.tanh(_dot(xw, w1_ref[d]))), w2_ref[d]))
        softplus = jnp.maximum(x, 0.0) + jnp.log(1.0 + jnp.exp(-jnp.abs(x)))
        lw_out[d, 0] = -jnp.exp(-softplus - 0.5)
        a = jax.nn.sigmoid(a0_ref[d:d + 1, :] + _dot(_mx(_dot(xa, a1_ref[d])), a2_ref[d]))
        g_out[d, 0] = _dot(_mx(jax.nn.sigmoid(_dot(xg, g1_ref[d]))), g2_ref[d])
        kx = k * kk_ref[d:d + 1, :]
        kn = kx / jnp.maximum(jnp.sqrt(_head_sum(kx * kx, e_ref)), 1e-12)
        kn_out[d, 0] = kn
        ba_out[d, 0] = kn * a
        kd_out[d, 0] = k * (1.0 + (a - 1.0) * ka_ref[d:d + 1, :])


def _rw_tok(h, mods, p, tm=128):
    b, t, d = h.shape
    tm = min(tm, t)
    nb8 = t // SUBLANES
    per8 = tm // SUBLANES
    tok = pl.BlockSpec((1, tm, d), lambda i, j: (i, j, 0))
    prev = pl.BlockSpec((1, SUBLANES, d), lambda i, j: (i, jnp.maximum(j * per8 - 1, 0), 0))
    nxt = pl.BlockSpec((1, SUBLANES, d), lambda i, j: (i, jnp.minimum((j + 1) * per8, nb8 - 1), 0))
    names = ("mu", "wr", "wk", "wv", "w0", "w1", "w2", "a0", "a1", "a2", "g1", "g2", "kk", "ka", "e")
    weights = [p[n] for n in names]
    both = pl.BlockSpec((2, 1, tm, d), lambda i, j: (0, i, j, 0))
    one = jax.ShapeDtypeStruct((b, t, d), F32)
    two = jax.ShapeDtypeStruct((2, b, t, d), F32)
    return pl.pallas_call(
        _rw_tok_kernel,
        grid=(b, t // tm),
        in_specs=[tok, prev, nxt, _mod_spec(mods, d)] + [_resident(w.shape) for w in weights],
        out_specs=[tok, tok, both, both, both, both, both],
        out_shape=[one, one, two, two, two, two, two],
        compiler_params=_cparams(2),
        name="rw_tok",
    )(h, h, h, mods, *weights)


def _mm(a, b, dims=_NN):
    return lax.dot_general(_mx(a), _mx(b), dims, preferred_element_type=F32)


def _split(x):
    hi = _mx(x)
    return hi, _mx(x - hi.astype(F32))


def _mm3(a, b, dims=_NN):
    ah, al = _split(a)
    bh, bl = _split(b)
    dg = functools.partial(lax.dot_general, dimension_numbers=dims, preferred_element_type=F32)
    return dg(ah, bh) + (dg(al, bh) + dg(ah, bl))


def _cumsum_mm(tri, x):
    h1, rest = _mx(x), x - _mx(x).astype(F32)
    h2, h3 = _split(rest)
    dg = functools.partial(lax.dot_general, dimension_numbers=_NN, preferred_element_type=F32)
    return dg(tri, h1) + (dg(tri, h2) + dg(tri, h3))


def _scan_masks(rev):
    L = SCAN_CHUNK
    r1 = lax.broadcasted_iota(jnp.int32, (L, LANES), 0)
    c1 = lax.broadcasted_iota(jnp.int32, (L, LANES), 1) & (L - 1)
    tr = lax.broadcasted_iota(jnp.int32, (L, L), 0)
    tc = lax.broadcasted_iota(jnp.int32, (L, L), 1)
    rt6 = lax.broadcasted_iota(jnp.int32, (2 * L, LANES), 0) & (L - 1)
    ct6 = lax.broadcasted_iota(jnp.int32, (2 * L, LANES), 1) & (L - 1)
    late, early = (ct6, rt6) if rev else (rt6, ct6)
    off = [((rt6 >> (i + 1)) == (ct6 >> (i + 1))) & ((late & (1 << i)) != 0) & ((early & (1 << i)) == 0)
           for i in range(int(math.log2(L)))]
    return dict(
        incl=(c1 >= r1) if rev else (c1 <= r1),
        strict=(c1 > r1) if rev else (c1 < r1),
        tri=jnp.where((tc >= tr) if rev else (tc <= tr), 1.0, 0.0).astype(MXU_DTYPE),
        off=off,
    )


def _scan_kernel(s0_ref, *refs):
    L = SCAN_CHUNK
    in_refs = (refs[0:6], refs[6:12])
    y_refs = refs[12:14]
    sf_ref, s_scr = refs[14:]
    n_pairs = s_scr.shape[1]
    n_chunks = in_refs[0][0].shape[2] // L

    @pl.when(pl.program_id(1) == 0)
    def _():
        s_scr[...] = s0_ref[:, 0]

    masks = (_scan_masks(False), _scan_masks(True))
    r2 = lax.broadcasted_iota(jnp.int32, (2 * L, LANES), 0)
    c2 = lax.broadcasted_iota(jnp.int32, (2 * L, LANES), 1)
    own = (r2 < L) == (c2 < L)
    eye = jnp.where(r2 == c2, 1.0, 0.0).astype(F32)
    chains = [(d, p) for d in range(2) for p in range(n_pairs)]
    mk = [masks[d] for d, _ in chains]

    def bd(x):
        return jnp.where(own, jnp.concatenate([x, x], axis=0), 0.0)

    def each(fn, *lists):
        return [fn(*args) for args in zip(*lists)]

    def chunk(ci, carry):
        rows = (pl.ds(pl.multiple_of(ci * L, L), L), pl.ds(pl.multiple_of((n_chunks - 1 - ci) * L, L), L))

        def load(which, lead):
            out = []
            for d, p in chains:
                ref = in_refs[d][which]
                idx = (0, 0) if lead else (0,)
                out.append(ref[idx + (rows[d], slice(p * LANES, (p + 1) * LANES))])
            return out

        lw = load(0, True)
        cin = each(lambda m, x: _cumsum_mm(m["tri"], x), mk, lw)
        e_in = each(jnp.exp, cin)
        e_neg = each(lambda c: jnp.exp(-c), cin)
        at = each(lambda k_, c, x: -k_ * jnp.exp(c - x), load(1, True), cin, lw)
        rt = each(lambda r_, e: r_ * e, load(4, False), e_in)
        bt = each(lambda b_, e: b_ * e, load(2, True), e_neg)
        kt = each(lambda k_, e: k_ * e, load(3, True), e_neg)
        wl = [e[0:1] if d else e[L - 1:L] for (d, _), e in zip(chains, e_in)]
        g = each(lambda a, r_, b_, k_: _mm(jnp.concatenate([a, r_], axis=0),
                                           jnp.concatenate([bd(b_), bd(k_)], axis=0), _NT), at, rt, bt, kt)
        n_bd = each(lambda m, x: bd(jnp.where(m["strict"], x[:L, :LANES], 0.0)), mk, g)
        arb = each(lambda m, x: bd(jnp.where(m["incl"], x[L:, :LANES], 0.0)), mk, g)
        aak = each(lambda m, x: bd(jnp.where(m["strict"], x[:L, LANES:], 0.0)), mk, g)
        ark = each(lambda m, x: bd(jnp.where(m["incl"], x[L:, LANES:], 0.0)), mk, g)
        tinv = each(lambda m, n: eye + jnp.where(m["off"][0], n, 0.0), mk, n_bd)
        for lvl in range(1, int(math.log2(L))):
            inner = each(lambda m, n, t_: _mm(jnp.where(m["off"][lvl], n, 0.0), t_), mk, n_bd, tinv)
            tinv = each(lambda t_, x: t_ + _mm(t_, x), tinv, inner)
        v_sd = each(bd, load(5, False))
        akv = each(_mm, aak, v_sd)
        pq = each(lambda t_, a, x: _mm(t_, jnp.concatenate([bd(a), x], axis=1)), tinv, at, akv)
        bh = each(lambda b_, w: bd(b_ * w), bt, wl)
        kh = each(lambda k_, w: bd(k_ * w), kt, wl)
        qv = each(lambda x, v_: jnp.concatenate([x[:, LANES:], v_], axis=0), pq, v_sd)
        ptb = each(lambda x, b_: _mm(x[:, :LANES].T, b_), pq, bh)
        c_mat = each(lambda x, b_, k_: _mm(x.T, jnp.concatenate([b_, k_], axis=0)), qv, bh, kh)
        ry = each(lambda r_, a, x: bd(r_) + _mm(a, x[:, :LANES]), rt, arb, pq)
        y0 = each(lambda a, k_, x: _mm(jnp.concatenate([a, k_], axis=1), x), arb, ark, qv)
        s_in = [s_scr[d, p] for d, p in chains]
        ysd = each(lambda r_, s_, y_: _mm3(r_, s_, _NT) + y_, ry, s_in, y0)
        for (d, p), y_ in zip(chains, ysd):
            y_refs[d][0, rows[d], p * LANES:(p + 1) * LANES] = y_[:L] + y_[L:]
        s_out = each(lambda s_, w, x, c: s_ * w + _mm3(s_, x) + c, s_in, wl, ptb, c_mat)
        for (d, p), s_ in zip(chains, s_out):
            s_scr[d, p] = s_
        return carry

    lax.fori_loop(0, n_chunks, chunk, 0)
    sf_ref[:, 0] = s_scr[...]


def _scan(s0, lw, kn, ba, kd, r, v, tb=256):
    b, t, d = r.shape
    tb = min(tb, t)
    nb = t // tb
    n_pairs = d // LANES
    state = pl.BlockSpec((2, 1, n_pairs, LANES, LANES), lambda i, j: (0, i, 0, 0, 0))
    in_specs = [state]
    args = [s0]
    out_specs = []
    for direction in range(2):
        tmap = (lambda j: nb - 1 - j) if direction else (lambda j: j)
        dirspec = pl.BlockSpec((1, 1, tb, d), lambda i, j, direction=direction, tmap=tmap: (direction, i, tmap(j), 0))
        tok = pl.BlockSpec((1, tb, d), lambda i, j, tmap=tmap: (i, tmap(j), 0))
        in_specs += [dirspec, dirspec, dirspec, dirspec, tok, tok]
        args += [lw, kn, ba, kd, r, v]
        out_specs.append(tok)
    out_specs.append(state)
    y = jax.ShapeDtypeStruct((b, t, d), F32)
    return pl.pallas_call(
        _scan_kernel,
        grid=(b, nb),
        in_specs=in_specs,
        out_specs=out_specs,
        out_shape=[y, y, jax.ShapeDtypeStruct((2, b, n_pairs, LANES, LANES), F32)],
        scratch_shapes=[pltpu.VMEM((2, n_pairs, LANES, LANES), F32)],
        compiler_params=_cparams(2),
        name="rw_scan",
    )(*args)


def _rw_out_kernel(h_ref, mod_ref, y0_ref, y1_ref, kd_ref, g_ref, r_ref, v_ref,
                   rk_ref, lng_ref, lnb_ref, e_ref, wo_ref, o_ref):
    r = r_ref[0]
    v = v_ref[0]
    inv_n = 1.0 / RW_HEAD
    total = jnp.zeros(r.shape, F32)
    for d, y_ref in enumerate((y0_ref, y1_ref)):
        y = y_ref[0]
        yc = y - _head_sum(y, e_ref) * inv_n
        var = _head_sum(yc * yc, e_ref) * inv_n
        yn = yc * lax.rsqrt(var + RW_GN_EPS)
        bonus = _head_sum(r * kd_ref[d, 0] * rk_ref[d:d + 1, :], e_ref) * v
        out = yn * lng_ref[d:d + 1, :] + lnb_ref[d:d + 1, :] + bonus
        total = total + out * g_ref[d, 0]
    gate = mod_ref[0, 5:6, :]
    o_ref[0] = h_ref[0] + gate * _dot(_mx(total), wo_ref[...])


def _rw_out(h, mods, y0, y1, kd, g, r, v, p, tm=256):
    b, t, d = h.shape
    tm = min(tm, t)
    tok = pl.BlockSpec((1, tm, d), lambda i, j: (i, j, 0))
    both = pl.BlockSpec((2, 1, tm, d), lambda i, j: (0, i, j, 0))
    weights = [p["rk"], p["ln_g"], p["ln_b"], p["e"], p["wo"]]
    return pl.pallas_call(
        _rw_out_kernel,
        grid=(b, t // tm),
        in_specs=[tok, _mod_spec(mods, d), tok, tok, both, both, tok, tok] + [_resident(w.shape) for w in weights],
        out_specs=tok,
        out_shape=jax.ShapeDtypeStruct((b, t, d), F32),
        compiler_params=_cparams(2),
        name="rw_out",
    )(h, mods, y0, y1, kd, g, r, v, *weights)


def _da_layer(h, hc, mod, modc, wqkv, wo, lam_vecs, subln, lam_init, need_ctx):
    s = h.shape[1]
    wqkv, wo = _mx(wqkv), _mx(wo)
    subln = subln.reshape(1, -1)
    q, k, v = _da_proj(h, mod, wqkv, _rope_tables(s, DA_HEAD_DIM))
    qc, kc, vc = _da_proj(hc, modc, wqkv, None)
    y = _da_attn(q, [(kc, vc), (k, v)], lam_vecs, subln, lam_init)
    h = _out_proj(h, mod, y, wo)
    if need_ctx:
        yc = _da_attn(qc, [(kc, vc)], lam_vecs, subln, lam_init)
        hc = _out_proj(hc, modc, yc, wo)
    return h, hc


def _gq_layer(h, hc, mod, modc, wqkv, wo, qk_g, need_ctx):
    s = h.shape[1]
    wqkv, wo = _mx(wqkv), _mx(wo)
    q, k, v = _gq_proj(h, mod, wqkv, qk_g, _rope_tables(s, GQ_HEAD_DIM))
    qc, kc, vc = _gq_proj(hc, modc, wqkv, qk_g, None)
    y = _gq_attn(q, [(kc, vc), (k, v)])
    h = _out_proj(h, mod, y, wo)
    if need_ctx:
        yc = _gq_attn(qc, [(kc, vc)])
        hc = _out_proj(hc, modc, yc, wo)
    return h, hc


def _rw_layer(h, hc, mod, modc, p, need_ctx):
    b, _, d = h.shape
    group = lax.broadcasted_iota(jnp.int32, (d, d), 0) // RW_HEAD == lax.broadcasted_iota(jnp.int32, (d, d), 1) // RW_HEAD
    p = dict(p, e=group.astype(MXU_DTYPE))
    r, v, lw, kd, kn, ba, g = _rw_tok(h, mod, p)
    rc, vc, lwc, kdc, knc, bac, gc = _rw_tok(hc, modc, p)
    zero = jnp.zeros((2, b, d // LANES, LANES, LANES), F32)
    yc0, yc1, state_c = _scan(zero, lwc, knc, bac, kdc, rc, vc)
    y0, y1, _ = _scan(state_c, lw, kn, ba, kd, r, v)
    h = _rw_out(h, mod, y0, y1, kd, g, r, v, p)
    if need_ctx:
        hc = _rw_out(hc, modc, yc0, yc1, kdc, gc, rc, vc, p)
    return h, hc


def kernel(x, c, ctx, c_ctx, ada_w, ada_b, ffn_wg, ffn_wu, ffn_wd, final_g, da_wqkv, da_wo, da_lam, da_subln, gq_wqkv, gq_wo, gq_qk_g, rw_mu, rw_wr, rw_wk, rw_wv, rw_wo, rw_w0, rw_w1, rw_w2, rw_a0, rw_a1, rw_a2, rw_g1, rw_g2, rw_kk, rw_ka, rw_rk, rw_ln_g, rw_ln_b):
    b, _, d = x.shape
    depth = ada_w.shape[0]
    rows = -(-(b + 1) // SUBLANES) * SUBLANES
    cvec = jnp.zeros((rows, d), F32).at[:b].set(c).at[b].set(c_ctx)
    mods = _adaln(cvec, ada_w, ada_b)
    h, hc = x, ctx
    for i in range(depth):
        kind, j = i % 3, i // 3
        need_ctx = i < depth - 1
        mod = mods[i, :b].reshape(b, N_MOD, d)
        modc = mods[i, b:b + 1].reshape(1, N_MOD, d)
        wg, wu, wd = _mx(ffn_wg[i, 0]), _mx(ffn_wu[i, 0]), _mx(ffn_wd[i, 0])
        h = _ffn(h, mod, wg, wu, wd, 0)
        hc = _ffn(hc, modc, wg, wu, wd, 0)
        if kind == 0:
            lam_init = 0.8 - 0.6 * math.exp(-0.3 * i)
            h, hc = _da_layer(h, hc, mod, modc, da_wqkv[j], da_wo[j], da_lam[j], da_subln[j], lam_init, need_ctx)
        elif kind == 1:
            h, hc = _gq_layer(h, hc, mod, modc, gq_wqkv[j], gq_wo[j], gq_qk_g[j], need_ctx)
        else:
            p = dict(mu=rw_mu[j], wr=_mx(rw_wr[j]), wk=_mx(rw_wk[j]), wv=_mx(rw_wv[j]), wo=_mx(rw_wo[j]),
                     w0=rw_w0[j], w1=_mx(rw_w1[j]), w2=_mx(rw_w2[j]), a0=rw_a0[j], a1=_mx(rw_a1[j]),
                     a2=_mx(rw_a2[j]), g1=_mx(rw_g1[j]), g2=_mx(rw_g2[j]), kk=rw_kk[j], ka=rw_ka[j],
                     rk=rw_rk[j].reshape(2, d), ln_g=rw_ln_g[j], ln_b=rw_ln_b[j])
            h, hc = _rw_layer(h, hc, mod, modc, p, need_ctx)
        wg, wu, wd = _mx(ffn_wg[i, 1]), _mx(ffn_wu[i, 1]), _mx(ffn_wd[i, 1])
        h = _ffn(h, mod, wg, wu, wd, 6, final_gain=final_g if i == depth - 1 else None)
        if need_ctx:
            hc = _ffn(hc, modc, wg, wu, wd, 6)
    return h
```

```python
import functools
import math

import jax
import jax.numpy as jnp
from jax import lax
from jax.experimental import pallas as pl
from jax.experimental.pallas import tpu as pltpu

F32 = jnp.float32
MXU_DTYPE = jnp.bfloat16

NORM_EPS = 1e-6
ROPE_THETA = 10000.0
GRID_W = 64
N_MOD = 9
DA_HEADS = 8
DA_HEAD_DIM = 64
GQ_HEADS = 8
GQ_KV_HEADS = 2
GQ_HEAD_DIM = 128
GQ_GROUP = GQ_HEADS // GQ_KV_HEADS
RW_HEAD = 64
RW_GN_EPS = 64e-5

LANES = 128
SUBLANES = 8
VMEM_LIMIT_BYTES = 56 * 1024 * 1024

SCAN_CHUNK = 64


def _cparams(n_axes):
    return pltpu.CompilerParams(
        dimension_semantics=("arbitrary",) * n_axes,
        vmem_limit_bytes=VMEM_LIMIT_BYTES,
    )


def _resident(shape):
    zeros = (0,) * len(shape)
    return pl.BlockSpec(shape, lambda *_: zeros)


def _mod_spec(mods, d):
    if mods.shape[0] > 1:
        return pl.BlockSpec((1, N_MOD, d), lambda b, t: (b, 0, 0))
    return pl.BlockSpec((1, N_MOD, d), lambda b, t: (0, 0, 0))


def _rms(x):
    return x * lax.rsqrt(jnp.mean(x * x, axis=-1, keepdims=True) + NORM_EPS)


def _modulate(x, mod_ref, base):
    shift = mod_ref[0, base:base + 1, :]
    scale = mod_ref[0, base + 1:base + 2, :]
    return _rms(x) * (1.0 + scale) + shift


def _dot(a, b):
    return jnp.dot(a, b, preferred_element_type=F32)


def _mx(x):
    return x.astype(MXU_DTYPE)


_NT = (((1,), (1,)), ((), ()))
_NN = (((1,), (0,)), ((), ()))
LOG2E = math.log2(math.e)


def _adaln_kernel(c_ref, w_ref, b_ref, o_ref):
    c = c_ref[...]
    act = _mx(c * jax.nn.sigmoid(c))
    o_ref[0] = _dot(act, _mx(w_ref[0])) + b_ref[0]


def _adaln(cvec, ada_w, ada_b):
    depth, d, n = ada_w.shape
    rows = cvec.shape[0]
    tn = 1024
    return pl.pallas_call(
        _adaln_kernel,
        grid=(depth, n // tn),
        in_specs=[
            pl.BlockSpec((rows, d), lambda i, j: (0, 0)),
            pl.BlockSpec((1, d, tn), lambda i, j: (i, 0, j)),
            pl.BlockSpec((1, 1, tn), lambda i, j: (i, 0, j)),
        ],
        out_specs=pl.BlockSpec((1, rows, tn), lambda i, j: (i, 0, j)),
        out_shape=jax.ShapeDtypeStruct((depth, rows, n), F32),
        compiler_params=_cparams(2),
        name="adaln",
    )(cvec, ada_w, ada_b.reshape(depth, 1, n))


def _ffn_kernel(h_ref, mod_ref, wg_ref, wu_ref, wd_ref, *rest, base, tf, mixer, final):
    rest = list(rest)
    o_ref = rest.pop()
    x = h_ref[0]
    if mixer:
        y_ref, wo_ref = rest[0], rest[1]
        x = x + mod_ref[0, 5:6, :] * _dot(y_ref[0], wo_ref[...])
    xn = _mx(_modulate(x, mod_ref, base))
    f = wg_ref.shape[1]
    acc = jnp.zeros(x.shape, F32)
    for c in range(f // tf):
        g = _dot(xn, wg_ref[:, c * tf:(c + 1) * tf])
        u = _dot(xn, wu_ref[:, c * tf:(c + 1) * tf])
        a = _mx(g * jax.nn.sigmoid(g) * u)
        acc = acc + _dot(a, wd_ref[c * tf:(c + 1) * tf, :])
    gate = mod_ref[0, base + 2:base + 3, :]
    out = x + (0.5 * gate) * acc
    if final:
        out = _rms(out) * rest[-1][...]
    o_ref[0] = out


def _ffn(h, mods, wg, wu, wd, base, mixer=None, final_gain=None, tm=512, tf=256):
    b, t, d = h.shape
    f = wg.shape[1]
    tm = min(tm, t)
    tok = pl.BlockSpec((1, tm, d), lambda i, j: (i, j, 0))
    in_specs = [tok, _mod_spec(mods, d), _resident((d, f)), _resident((d, f)), _resident((f, d))]
    args = [h, mods, wg, wu, wd]
    if mixer is not None:
        in_specs += [tok, _resident(mixer[1].shape)]
        args += list(mixer)
    if final_gain is not None:
        in_specs.append(_resident((1, d)))
        args.append(final_gain.reshape(1, d))
    return pl.pallas_call(
        functools.partial(_ffn_kernel, base=base, tf=tf, mixer=mixer is not None, final=final_gain is not None),
        grid=(b, t // tm),
        in_specs=in_specs,
        out_specs=tok,
        out_shape=jax.ShapeDtypeStruct((b, t, d), F32),
        compiler_params=_cparams(2),
        name="ffn",
    )(*args)


def _rope_tables(n_tokens, head_dim):
    rows = n_tokens // GRID_W
    row = jnp.repeat(jnp.arange(rows, dtype=jnp.int32), GRID_W).astype(F32)
    col = jnp.tile(jnp.arange(GRID_W, dtype=jnp.int32), rows).astype(F32)
    n_freq = head_dim // 4
    inv = jnp.power(ROPE_THETA, -jnp.arange(n_freq, dtype=F32) / n_freq)
    ang = jnp.concatenate([row[:, None] * inv, col[:, None] * inv], axis=-1)
    cos, sin = jnp.cos(ang), jnp.sin(ang)
    reps = LANES // head_dim
    cos_t = jnp.tile(jnp.concatenate([cos, cos], axis=-1), (1, reps))
    sin_t = jnp.tile(jnp.concatenate([-sin, sin], axis=-1), (1, reps))
    return cos_t, sin_t


def _rope_slab(x, cos, sin, half):
    if 2 * half == LANES:
        swapped = pltpu.roll(x, half, 1)
    else:
        lane = lax.broadcasted_iota(jnp.int32, x.shape, 1)
        first = (lane & (2 * half - 1)) < half
        swapped = jnp.where(first, pltpu.roll(x, LANES - half, 1), pltpu.roll(x, half, 1))
    return x * cos + swapped * sin


def _da_proj_kernel(h_ref, mod_ref, w_ref, *rest, rope):
    if rope:
        cos_ref, sin_ref, q_ref, k_ref, v_ref = rest
        cos, sin = cos_ref[...], sin_ref[...]
    else:
        q_ref, k_ref, v_ref = rest
    d = h_ref.shape[2]
    u = _mx(_modulate(h_ref[0], mod_ref, 3))
    scale = DA_HEAD_DIM ** -0.5 * LOG2E
    for idx, (o_ref, rot, mul) in enumerate(((q_ref, True, scale), (k_ref, True, 1.0), (v_ref, False, 1.0))):
        y = _dot(u, w_ref[:, idx * d:(idx + 1) * d])
        for j in range(d // LANES):
            ys = y[:, j * LANES:(j + 1) * LANES]
            if rot and rope:
                ys = _rope_slab(ys, cos, sin, DA_HEAD_DIM // 2)
            if mul != 1.0:
                ys = ys * mul
            o_ref[0, :, j * LANES:(j + 1) * LANES] = ys.astype(o_ref.dtype)


def _da_proj(h, mods, wqkv, tables, tm=512):
    b, t, d = h.shape
    tm = min(tm, t)
    rope = tables is not None
    tok = pl.BlockSpec((1, tm, d), lambda i, j: (i, j, 0))
    in_specs = [tok, _mod_spec(mods, d), _resident(wqkv.shape)]
    args = [h, mods, wqkv]
    if rope:
        tab = pl.BlockSpec((tm, LANES), lambda i, j: (j, 0))
        in_specs += [tab, tab]
        args += list(tables)
    out = jax.ShapeDtypeStruct((b, t, d), MXU_DTYPE)
    return pl.pallas_call(
        functools.partial(_da_proj_kernel, rope=rope),
        grid=(b, t // tm),
        in_specs=in_specs,
        out_specs=[tok, tok, tok],
        out_shape=[out, out, out],
        compiler_params=_cparams(2),
        name="da_proj",
    )(*args)


def _gq_proj_kernel(h_ref, mod_ref, w_ref, g_ref, *rest, rope):
    if rope:
        cos_ref, sin_ref, q_ref, k_ref, v_ref = rest
        cos, sin = cos_ref[...], sin_ref[...]
    else:
        q_ref, k_ref, v_ref = rest
    u = _mx(_modulate(h_ref[0], mod_ref, 3))
    nq = GQ_HEADS * GQ_HEAD_DIM
    nk = GQ_KV_HEADS * GQ_HEAD_DIM
    scale = GQ_HEAD_DIM ** -0.5 * LOG2E
    plan = ((q_ref, 0, nq, 0, scale), (k_ref, nq, nk, 1, 1.0), (v_ref, nq + nk, nk, None, 1.0))
    for o_ref, start, width, gain_row, mul in plan:
        y = _dot(u, w_ref[:, start:start + width])
        for j in range(width // LANES):
            ys = y[:, j * LANES:(j + 1) * LANES]
            if gain_row is not None:
                ys = _rms(ys) * g_ref[gain_row:gain_row + 1, :]
                if rope:
                    ys = _rope_slab(ys, cos, sin, GQ_HEAD_DIM // 2)
            if mul != 1.0:
                ys = ys * mul
            o_ref[0, :, j * LANES:(j + 1) * LANES] = ys.astype(o_ref.dtype)


def _gq_proj(h, mods, wqkv, qk_g, tables, tm=512):
    b, t, d = h.shape
    tm = min(tm, t)
    rope = tables is not None
    nq = GQ_HEADS * GQ_HEAD_DIM
    nk = GQ_KV_HEADS * GQ_HEAD_DIM
    tok = pl.BlockSpec((1, tm, d), lambda i, j: (i, j, 0))
    in_specs = [tok, _mod_spec(mods, d), _resident(wqkv.shape), _resident(qk_g.shape)]
    args = [h, mods, wqkv, qk_g]
    if rope:
        tab = pl.BlockSpec((tm, LANES), lambda i, j: (j, 0))
        in_specs += [tab, tab]
        args += list(tables)
    return pl.pallas_call(
        functools.partial(_gq_proj_kernel, rope=rope),
        grid=(b, t // tm),
        in_specs=in_specs,
        out_specs=[pl.BlockSpec((1, tm, nq), lambda i, j: (i, j, 0)),
                   pl.BlockSpec((1, tm, nk), lambda i, j: (i, j, 0)),
                   pl.BlockSpec((1, tm, nk), lambda i, j: (i, j, 0))],
        out_shape=[jax.ShapeDtypeStruct((b, t, nq), MXU_DTYPE),
                   jax.ShapeDtypeStruct((b, t, nk), MXU_DTYPE),
                   jax.ShapeDtypeStruct((b, t, nk), MXU_DTYPE)],
        compiler_params=_cparams(2),
        name="gq_proj",
    )(*args)


def _attn_pipeline(n_blocks, k_refs, s_scr, make_qs, finish):
    def scores(i):
        qs = make_qs(i)
        off = 0
        for k_ref in k_refs:
            n = k_ref.shape[1]
            s_scr[i % 2, :, off:off + n] = lax.dot_general(qs, k_ref[0], _NT, preferred_element_type=F32)
            off += n

    scores(0)
    for i in range(n_blocks):
        if i + 1 < n_blocks:
            scores(i + 1)
        s = s_scr[i % 2]
        p = jnp.exp2(s - jnp.max(s, axis=-1, keepdims=True))
        finish(i, p, jnp.sum(p, axis=-1, keepdims=True))


def _pv(a, v_refs):
    out, off = None, 0
    for v_ref in v_refs:
        n = v_ref.shape[1]
        part = _dot(a[:, off:off + n], v_ref[0])
        out = part if out is None else out + part
        off += n
    return out


def _da_attn_kernel(lam_ref, g_ref, q_ref, *rest, nseg, rb, lam_init):
    k_refs = [rest[2 * i] for i in range(nseg)]
    v_refs = [rest[2 * i + 1] for i in range(nseg)]
    o_ref, s_scr = rest[2 * nseg:]
    lv = lam_ref[...]
    lam = (jnp.exp(jnp.sum(lv[0:1] * lv[1:2], keepdims=True))
           - jnp.exp(jnp.sum(lv[2:3] * lv[3:4], keepdims=True)) + lam_init)
    gain = g_ref[...] * (1.0 - lam_init)
    lane = lax.broadcasted_iota(jnp.int32, (rb, LANES), 1)

    def make_qs(i):
        q = q_ref[0, i * rb:(i + 1) * rb, :]
        zero = jnp.zeros_like(q)
        return jnp.concatenate([jnp.where(lane < DA_HEAD_DIM, q, zero), jnp.where(lane >= DA_HEAD_DIM, q, zero)], axis=0)

    def finish(i, p, l):
        l1, l2 = l[:rb], l[rb:]
        ratio = lam * l1 / l2
        o = _pv(_mx(p[:rb] - ratio * p[rb:]), v_refs)
        y = _rms(o / l1) * gain
        o_ref[0, i * rb:(i + 1) * rb, :] = y.astype(o_ref.dtype)

    _attn_pipeline(q_ref.shape[1] // rb, k_refs, s_scr, make_qs, finish)


def _da_attn(q, kv_segs, lam_vecs, subln, lam_init, tq=1024, rb=128):
    b, t, d = q.shape
    tq = min(tq, t)
    rb = min(rb, tq)
    in_specs = [_resident(lam_vecs.shape), _resident(subln.shape),
                pl.BlockSpec((1, tq, LANES), lambda i, h, j: (i, j, h))]
    args = [lam_vecs, subln, q]
    n_keys = 0
    for k, v in kv_segs:
        n = k.shape[1]
        n_keys += n
        seg = pl.BlockSpec((1, n, LANES), lambda i, h, j: (i, 0, h))
        in_specs += [seg, seg]
        args += [k, v]
    return pl.pallas_call(
        functools.partial(_da_attn_kernel, nseg=len(kv_segs), rb=rb, lam_init=lam_init),
        grid=(b, DA_HEADS, t // tq),
        in_specs=in_specs,
        out_specs=pl.BlockSpec((1, tq, LANES), lambda i, h, j: (i, j, h)),
        out_shape=jax.ShapeDtypeStruct((b, t, d), MXU_DTYPE),
        scratch_shapes=[pltpu.VMEM((2, 2 * rb, n_keys), F32)],
        compiler_params=_cparams(3),
        name="da_attn",
    )(*args)


def _gq_attn_kernel(q_ref, *rest, nseg, rb):
    k_refs = [rest[2 * i] for i in range(nseg)]
    v_refs = [rest[2 * i + 1] for i in range(nseg)]
    o_ref, s_scr = rest[2 * nseg:]

    def make_qs(i):
        return jnp.concatenate([q_ref[0, i * rb:(i + 1) * rb, m * LANES:(m + 1) * LANES] for m in range(GQ_GROUP)], axis=0)

    def finish(i, p, l):
        o = _pv(_mx(p), v_refs) / l
        for m in range(GQ_GROUP):
            o_ref[0, i * rb:(i + 1) * rb, m * LANES:(m + 1) * LANES] = o[m * rb:(m + 1) * rb].astype(o_ref.dtype)

    _attn_pipeline(q_ref.shape[1] // rb, k_refs, s_scr, make_qs, finish)


def _gq_attn(q, kv_segs, tq=512, rb=32):
    b, t, d = q.shape
    tq = min(tq, t)
    rb = min(rb, tq)
    gw = GQ_GROUP * GQ_HEAD_DIM
    in_specs = [pl.BlockSpec((1, tq, gw), lambda i, g, j: (i, j, g))]
    args = [q]
    n_keys = 0
    for k, v in kv_segs:
        n = k.shape[1]
        n_keys += n
        seg = pl.BlockSpec((1, n, LANES), lambda i, g, j: (i, 0, g))
        in_specs += [seg, seg]
        args += [k, v]
    return pl.pallas_call(
        functools.partial(_gq_attn_kernel, nseg=len(kv_segs), rb=rb),
        grid=(b, GQ_KV_HEADS, t // tq),
        in_specs=in_specs,
        out_specs=pl.BlockSpec((1, tq, gw), lambda i, g, j: (i, j, g)),
        out_shape=jax.ShapeDtypeStruct((b, t, d), MXU_DTYPE),
        scratch_shapes=[pltpu.VMEM((2, GQ_GROUP * rb, n_keys), F32)],
        compiler_params=_cparams(3),
        name="gq_attn",
    )(*args)


def _head_sum(x, e_ref):
    return _dot(_mx(x), e_ref[...])


def _rw_tok_kernel(h_ref, hp_ref, hn_ref, mod_ref, mu_ref, wr_ref, wk_ref, wv_ref,
                   w0_ref, w1_ref, w2_ref, a0_ref, a1_ref, a2_ref, g1_ref, g2_ref, kk_ref, ka_ref, e_ref,
                   r_out, v_out, lw_out, kd_out, kn_out, ba_out, g_out):
    t = pl.program_id(1)
    nt = pl.num_programs(1)
    tm = h_ref.shape[1]
    z = _modulate(h_ref[0], mod_ref, 3)
    zp = _modulate(hp_ref[0], mod_ref, 3)[SUBLANES - 1:SUBLANES]
    zn = _modulate(hn_ref[0], mod_ref, 3)[0:1]
    zp = jnp.where(t > 0, zp, jnp.zeros_like(zp))
    zn = jnp.where(t < nt - 1, zn, jnp.zeros_like(zn))
    row = lax.broadcasted_iota(jnp.int32, z.shape, 0)
    dp = jnp.where(row == 0, zp, pltpu.roll(z, 1, 0)) - z
    dn = jnp.where(row == tm - 1, zn, pltpu.roll(z, tm - 1, 0)) - z

    def mix(i):
        return z + dp * mu_ref[0, i:i + 1, :] + dn * mu_ref[1, i:i + 1, :]

    r = _dot(_mx(mix(0)), wr_ref[...])
    k = _dot(_mx(mix(2)), wk_ref[...])
    v = _dot(_mx(mix(3)), wv_ref[...])
    xw, xa, xg = _mx(mix(1)), _mx(mix(4)), _mx(mix(5))
    r_out[0] = r
    v_out[0] = v
    for d in range(2):
        x = -(w0_ref[d:d + 1, :] + _dot(_mx(jnp.tanh(_dot(xw, w1_ref[d]))), w2_ref[d]))
        softplus = jnp.maximum(x, 0.0) + jnp.log(1.0 + jnp.exp(-jnp.abs(x)))
        lw_out[d, 0] = -jnp.exp(-softplus - 0.5)
        a = jax.nn.sigmoid(a0_ref[d:d + 1, :] + _dot(_mx(_dot(xa, a1_ref[d])), a2_ref[d]))
        g_out[d, 0] = _dot(_mx(jax.nn.sigmoid(_dot(xg, g1_ref[d]))), g2_ref[d])
        kx = k * kk_ref[d:d + 1, :]
        kn = kx / jnp.maximum(jnp.sqrt(_head_sum(kx * kx, e_ref)), 1e-12)
        kn_out[d, 0] = kn
        ba_out[d, 0] = kn * a
        kd_out[d, 0] = k * (1.0 + (a - 1.0) * ka_ref[d:d + 1, :])


def _rw_tok(h, mods, p, tm=128):
    b, t, d = h.shape
    tm = min(tm, t)
    nb8 = t // SUBLANES
    per8 = tm // SUBLANES
    tok = pl.BlockSpec((1, tm, d), lambda i, j: (i, j, 0))
    prev = pl.BlockSpec((1, SUBLANES, d), lambda i, j: (i, jnp.maximum(j * per8 - 1, 0), 0))
    nxt = pl.BlockSpec((1, SUBLANES, d), lambda i, j: (i, jnp.minimum((j + 1) * per8, nb8 - 1), 0))
    names = ("mu", "wr", "wk", "wv", "w0", "w1", "w2", "a0", "a1", "a2", "g1", "g2", "kk", "ka", "e")
    weights = [p[n] for n in names]
    both = pl.BlockSpec((2, 1, tm, d), lambda i, j: (0, i, j, 0))
    one = jax.ShapeDtypeStruct((b, t, d), F32)
    two = jax.ShapeDtypeStruct((2, b, t, d), F32)
    return pl.pallas_call(
        _rw_tok_kernel,
        grid=(b, t // tm),
        in_specs=[tok, prev, nxt, _mod_spec(mods, d)] + [_resident(w.shape) for w in weights],
        out_specs=[tok, tok, both, both, both, both, both],
        out_shape=[one, one, two, two, two, two, two],
        compiler_params=_cparams(2),
        name="rw_tok",
    )(h, h, h, mods, *weights)


def _mm(a, b, dims=_NN):
    return lax.dot_general(_mx(a), _mx(b), dims, preferred_element_type=F32)


def _split(x):
    hi = _mx(x)
    return hi, _mx(x - hi.astype(F32))


def _mm_state(a, s, dims=_NN, state_first=False):
    sh, sl = _split(s)
    a = _mx(a)
    dg = functools.partial(lax.dot_general, dimension_numbers=dims, preferred_element_type=F32)
    if state_first:
        return dg(sh, a) + dg(sl, a)
    return dg(a, sh) + dg(a, sl)


def _cumsum_mm(tri, x):
    h1, rest = _mx(x), x - _mx(x).astype(F32)
    h2, h3 = _split(rest)
    dg = functools.partial(lax.dot_general, dimension_numbers=_NN, preferred_element_type=F32)
    return dg(tri, h1) + (dg(tri, h2) + dg(tri, h3))


def _scan_masks(rev):
    L = SCAN_CHUNK
    r1 = lax.broadcasted_iota(jnp.int32, (L, LANES), 0)
    c1 = lax.broadcasted_iota(jnp.int32, (L, LANES), 1) & (L - 1)
    tr = lax.broadcasted_iota(jnp.int32, (L, L), 0)
    tc = lax.broadcasted_iota(jnp.int32, (L, L), 1)
    rt6 = lax.broadcasted_iota(jnp.int32, (2 * L, LANES), 0) & (L - 1)
    ct6 = lax.broadcasted_iota(jnp.int32, (2 * L, LANES), 1) & (L - 1)
    late, early = (ct6, rt6) if rev else (rt6, ct6)
    off = [((rt6 >> (i + 1)) == (ct6 >> (i + 1))) & ((late & (1 << i)) != 0) & ((early & (1 << i)) == 0)
           for i in range(int(math.log2(L)))]
    return dict(
        incl=(c1 >= r1) if rev else (c1 <= r1),
        strict=(c1 > r1) if rev else (c1 < r1),
        tri=jnp.where((tc >= tr) if rev else (tc <= tr), 1.0, 0.0).astype(MXU_DTYPE),
        off=off,
    )


def _scan_kernel(s0_ref, *refs):
    L = SCAN_CHUNK
    in_refs = (refs[0:6], refs[6:12])
    y_refs = refs[12:14]
    sf_ref, s_scr = refs[14:]
    n_pairs = s_scr.shape[1]
    n_chunks = in_refs[0][0].shape[2] // L

    @pl.when(pl.program_id(1) == 0)
    def _():
        s_scr[...] = s0_ref[:, 0]

    masks = (_scan_masks(False), _scan_masks(True))
    r2 = lax.broadcasted_iota(jnp.int32, (2 * L, LANES), 0)
    c2 = lax.broadcasted_iota(jnp.int32, (2 * L, LANES), 1)
    own = (r2 < L) == (c2 < L)
    eye = jnp.where(r2 == c2, 1.0, 0.0).astype(F32)
    chains = [(d, p) for d in range(2) for p in range(n_pairs)]
    mk = [masks[d] for d, _ in chains]

    def bd(x):
        return jnp.where(own, jnp.concatenate([x, x], axis=0), 0.0)

    def each(fn, *lists):
        return [fn(*args) for args in zip(*lists)]

    def chunk(ci, carry):
        rows = (pl.ds(pl.multiple_of(ci * L, L), L), pl.ds(pl.multiple_of((n_chunks - 1 - ci) * L, L), L))

        def load(which, lead):
            out = []
            for d, p in chains:
                ref = in_refs[d][which]
                idx = (0, 0) if lead else (0,)
                out.append(ref[idx + (rows[d], slice(p * LANES, (p + 1) * LANES))])
            return out

        lw = load(0, True)
        cin = each(lambda m, x: _cumsum_mm(m["tri"], x), mk, lw)
        e_in = each(jnp.exp, cin)
        e_neg = each(lambda c: jnp.exp(-c), cin)
        at = each(lambda k_, c, x: -k_ * jnp.exp(c - x), load(1, True), cin, lw)
        rt = each(lambda r_, e: r_ * e, load(4, False), e_in)
        bt = each(lambda b_, e: b_ * e, load(2, True), e_neg)
        kt = each(lambda k_, e: k_ * e, load(3, True), e_neg)
        wl = [e[0:1] if d else e[L - 1:L] for (d, _), e in zip(chains, e_in)]
        g = each(lambda a, r_, b_, k_: _mm(jnp.concatenate([a, r_], axis=0),
                                           jnp.concatenate([bd(b_), bd(k_)], axis=0), _NT), at, rt, bt, kt)
        n_bd = each(lambda m, x: bd(jnp.where(m["strict"], x[:L, :LANES], 0.0)), mk, g)
        arb = each(lambda m, x: bd(jnp.where(m["incl"], x[L:, :LANES], 0.0)), mk, g)
        aak = each(lambda m, x: bd(jnp.where(m["strict"], x[:L, LANES:], 0.0)), mk, g)
        ark = each(lambda m, x: bd(jnp.where(m["incl"], x[L:, LANES:], 0.0)), mk, g)
        tinv = each(lambda m, n: eye + jnp.where(m["off"][0], n, 0.0), mk, n_bd)
        for lvl in range(1, int(math.log2(L))):
            inner = each(lambda m, n, t_: _mm(jnp.where(m["off"][lvl], n, 0.0), t_), mk, n_bd, tinv)
            tinv = each(lambda t_, x: t_ + _mm(t_, x), tinv, inner)
        v_sd = each(bd, load(5, False))
        akv = each(_mm, aak, v_sd)
        pq = each(lambda t_, a, x: _mm(t_, jnp.concatenate([bd(a), x], axis=1)), tinv, at, akv)
        bh = each(lambda b_, w: bd(b_ * w), bt, wl)
        kh = each(lambda k_, w: bd(k_ * w), kt, wl)
        qv = each(lambda x, v_: jnp.concatenate([x[:, LANES:], v_], axis=0), pq, v_sd)
        ptb = each(lambda x, b_: _mm(x[:, :LANES].T, b_), pq, bh)
        c_mat = each(lambda x, b_, k_: _mm(x.T, jnp.concatenate([b_, k_], axis=0)), qv, bh, kh)
        ry = each(lambda r_, a, x: bd(r_) + _mm(a, x[:, :LANES]), rt, arb, pq)
        y0 = each(lambda a, k_, x: _mm(jnp.concatenate([a, k_], axis=1), x), arb, ark, qv)
        s_in = [s_scr[d, p] for d, p in chains]
        ysd = each(lambda r_, s_, y_: _mm_state(r_, s_, _NT) + y_, ry, s_in, y0)
        for (d, p), y_ in zip(chains, ysd):
            y_refs[d][0, rows[d], p * LANES:(p + 1) * LANES] = y_[:L] + y_[L:]
        s_out = each(lambda s_, w, x, c: s_ * w + _mm_state(x, s_, state_first=True) + c, s_in, wl, ptb, c_mat)
        for (d, p), s_ in zip(chains, s_out):
            s_scr[d, p] = s_
        return carry

    lax.fori_loop(0, n_chunks, chunk, 0)
    sf_ref[:, 0] = s_scr[...]


def _scan(s0, lw, kn, ba, kd, r, v, tb=256):
    b, t, d = r.shape
    tb = min(tb, t)
    nb = t // tb
    n_pairs = d // LANES
    state = pl.BlockSpec((2, 1, n_pairs, LANES, LANES), lambda i, j: (0, i, 0, 0, 0))
    in_specs = [state]
    args = [s0]
    out_specs = []
    for direction in range(2):
        tmap = (lambda j: nb - 1 - j) if direction else (lambda j: j)
        dirspec = pl.BlockSpec((1, 1, tb, d), lambda i, j, direction=direction, tmap=tmap: (direction, i, tmap(j), 0))
        tok = pl.BlockSpec((1, tb, d), lambda i, j, tmap=tmap: (i, tmap(j), 0))
        in_specs += [dirspec, dirspec, dirspec, dirspec, tok, tok]
        args += [lw, kn, ba, kd, r, v]
        out_specs.append(tok)
    out_specs.append(state)
    y = jax.ShapeDtypeStruct((b, t, d), F32)
    return pl.pallas_call(
        _scan_kernel,
        grid=(b, nb),
        in_specs=in_specs,
        out_specs=out_specs,
        out_shape=[y, y, jax.ShapeDtypeStruct((2, b, n_pairs, LANES, LANES), F32)],
        scratch_shapes=[pltpu.VMEM((2, n_pairs, LANES, LANES), F32)],
        compiler_params=_cparams(2),
        name="rw_scan",
    )(*args)


def _rw_out_kernel(h_ref, mod_ref, y0_ref, y1_ref, kd_ref, g_ref, r_ref, v_ref,
                   rk_ref, lng_ref, lnb_ref, e_ref, wo_ref, o_ref):
    r = r_ref[0]
    v = v_ref[0]
    inv_n = 1.0 / RW_HEAD
    total = jnp.zeros(r.shape, F32)
    for d, y_ref in enumerate((y0_ref, y1_ref)):
        y = y_ref[0]
        yc = y - _head_sum(y, e_ref) * inv_n
        var = _head_sum(yc * yc, e_ref) * inv_n
        yn = yc * lax.rsqrt(var + RW_GN_EPS)
        bonus = _head_sum(r * kd_ref[d, 0] * rk_ref[d:d + 1, :], e_ref) * v
        out = yn * lng_ref[d:d + 1, :] + lnb_ref[d:d + 1, :] + bonus
        total = total + out * g_ref[d, 0]
    gate = mod_ref[0, 5:6, :]
    o_ref[0] = h_ref[0] + gate * _dot(_mx(total), wo_ref[...])


def _rw_out(h, mods, y0, y1, kd, g, r, v, p, tm=256):
    b, t, d = h.shape
    tm = min(tm, t)
    tok = pl.BlockSpec((1, tm, d), lambda i, j: (i, j, 0))
    both = pl.BlockSpec((2, 1, tm, d), lambda i, j: (0, i, j, 0))
    weights = [p["rk"], p["ln_g"], p["ln_b"], p["e"], p["wo"]]
    return pl.pallas_call(
        _rw_out_kernel,
        grid=(b, t // tm),
        in_specs=[tok, _mod_spec(mods, d), tok, tok, both, both, tok, tok] + [_resident(w.shape) for w in weights],
        out_specs=tok,
        out_shape=jax.ShapeDtypeStruct((b, t, d), F32),
        compiler_params=_cparams(2),
        name="rw_out",
    )(h, mods, y0, y1, kd, g, r, v, *weights)


def _da_layer(h, hc, mod, modc, wqkv, wo, lam_vecs, subln, lam_init, need_ctx):
    s = h.shape[1]
    wqkv, wo = _mx(wqkv), _mx(wo)
    subln = subln.reshape(1, -1)
    q, k, v = _da_proj(h, mod, wqkv, _rope_tables(s, DA_HEAD_DIM))
    qc, kc, vc = _da_proj(hc, modc, wqkv, None)
    y = _da_attn(q, [(kc, vc), (k, v)], lam_vecs, subln, lam_init)
    yc = _da_attn(qc, [(kc, vc)], lam_vecs, subln, lam_init) if need_ctx else None
    return (y, wo), (yc, wo)


def _gq_layer(h, hc, mod, modc, wqkv, wo, qk_g, need_ctx):
    s = h.shape[1]
    wqkv, wo = _mx(wqkv), _mx(wo)
    q, k, v = _gq_proj(h, mod, wqkv, qk_g, _rope_tables(s, GQ_HEAD_DIM))
    qc, kc, vc = _gq_proj(hc, modc, wqkv, qk_g, None)
    y = _gq_attn(q, [(kc, vc), (k, v)])
    yc = _gq_attn(qc, [(kc, vc)]) if need_ctx else None
    return (y, wo), (yc, wo)


def _rw_layer(h, hc, mod, modc, p, need_ctx):
    b, _, d = h.shape
    group = lax.broadcasted_iota(jnp.int32, (d, d), 0) // RW_HEAD == lax.broadcasted_iota(jnp.int32, (d, d), 1) // RW_HEAD
    p = dict(p, e=group.astype(MXU_DTYPE))
    r, v, lw, kd, kn, ba, g = _rw_tok(h, mod, p)
    rc, vc, lwc, kdc, knc, bac, gc = _rw_tok(hc, modc, p)
    zero = jnp.zeros((2, b, d // LANES, LANES, LANES), F32)
    yc0, yc1, state_c = _scan(zero, lwc, knc, bac, kdc, rc, vc)
    y0, y1, _ = _scan(state_c, lw, kn, ba, kd, r, v)
    h = _rw_out(h, mod, y0, y1, kd, g, r, v, p)
    if need_ctx:
        hc = _rw_out(hc, modc, yc0, yc1, kdc, gc, rc, vc, p)
    return h, hc


def kernel(x, c, ctx, c_ctx, ada_w, ada_b, ffn_wg, ffn_wu, ffn_wd, final_g, da_wqkv, da_wo, da_lam, da_subln, gq_wqkv, gq_wo, gq_qk_g, rw_mu, rw_wr, rw_wk, rw_wv, rw_wo, rw_w0, rw_w1, rw_w2, rw_a0, rw_a1, rw_a2, rw_g1, rw_g2, rw_kk, rw_ka, rw_rk, rw_ln_g, rw_ln_b):
    b, _, d = x.shape
    depth = ada_w.shape[0]
    rows = -(-(b + 1) // SUBLANES) * SUBLANES
    cvec = jnp.zeros((rows, d), F32).at[:b].set(c).at[b].set(c_ctx)
    mods = _adaln(cvec, ada_w, ada_b)
    h, hc = x, ctx
    for i in range(depth):
        kind, j = i % 3, i // 3
        need_ctx = i < depth - 1
        mod = mods[i, :b].reshape(b, N_MOD, d)
        modc = mods[i, b:b + 1].reshape(1, N_MOD, d)
        wg, wu, wd = _mx(ffn_wg[i, 0]), _mx(ffn_wu[i, 0]), _mx(ffn_wd[i, 0])
        h = _ffn(h, mod, wg, wu, wd, 0)
        hc = _ffn(hc, modc, wg, wu, wd, 0)
        mixer = mixer_c = None
        if kind == 0:
            lam_init = 0.8 - 0.6 * math.exp(-0.3 * i)
            mixer, mixer_c = _da_layer(h, hc, mod, modc, da_wqkv[j], da_wo[j], da_lam[j], da_subln[j], lam_init, need_ctx)
        elif kind == 1:
            mixer, mixer_c = _gq_layer(h, hc, mod, modc, gq_wqkv[j], gq_wo[j], gq_qk_g[j], need_ctx)
        else:
            p = dict(mu=rw_mu[j], wr=_mx(rw_wr[j]), wk=_mx(rw_wk[j]), wv=_mx(rw_wv[j]), wo=_mx(rw_wo[j]),
                     w0=rw_w0[j], w1=_mx(rw_w1[j]), w2=_mx(rw_w2[j]), a0=rw_a0[j], a1=_mx(rw_a1[j]),
                     a2=_mx(rw_a2[j]), g1=_mx(rw_g1[j]), g2=_mx(rw_g2[j]), kk=rw_kk[j], ka=rw_ka[j],
                     rk=rw_rk[j].reshape(2, d), ln_g=rw_ln_g[j], ln_b=rw_ln_b[j])
            h, hc = _rw_layer(h, hc, mod, modc, p, need_ctx)
        wg, wu, wd = _mx(ffn_wg[i, 1]), _mx(ffn_wu[i, 1]), _mx(ffn_wd[i, 1])
        h = _ffn(h, mod, wg, wu, wd, 6, mixer=mixer, final_gain=final_g if i == depth - 1 else None)
        if need_ctx:
            hc = _ffn(hc, modc, wg, wu, wd, 6, mixer=mixer_c)
    return h
```

```python
import functools
import math

import jax
import jax.numpy as jnp
from jax import lax
from jax.experimental import pallas as pl
from jax.experimental.pallas import tpu as pltpu

F32 = jnp.float32
MXU_DTYPE = jnp.bfloat16

NORM_EPS = 1e-6
ROPE_THETA = 10000.0
GRID_W = 64
N_MOD = 9
DA_HEADS = 8
DA_HEAD_DIM = 64
GQ_HEADS = 8
GQ_KV_HEADS = 2
GQ_HEAD_DIM = 128
GQ_GROUP = GQ_HEADS // GQ_KV_HEADS
RW_HEAD = 64
RW_GN_EPS = 64e-5

LANES = 128
SUBLANES = 8
VMEM_LIMIT_BYTES = 56 * 1024 * 1024

SCAN_CHUNK = 64


def _cparams(n_axes):
    return pltpu.CompilerParams(
        dimension_semantics=("arbitrary",) * n_axes,
        vmem_limit_bytes=VMEM_LIMIT_BYTES,
    )


def _resident(shape):
    zeros = (0,) * len(shape)
    return pl.BlockSpec(shape, lambda *_: zeros)


def _mod_spec(mods, d):
    if mods.shape[0] > 1:
        return pl.BlockSpec((1, N_MOD, d), lambda b, t: (b, 0, 0))
    return pl.BlockSpec((1, N_MOD, d), lambda b, t: (0, 0, 0))


def _rms(x):
    return x * lax.rsqrt(jnp.mean(x * x, axis=-1, keepdims=True) + NORM_EPS)


def _modulate(x, mod_ref, base):
    shift = mod_ref[0, base:base + 1, :]
    scale = mod_ref[0, base + 1:base + 2, :]
    return _rms(x) * (1.0 + scale) + shift


def _dot(a, b):
    return jnp.dot(a, b, preferred_element_type=F32)


def _mx(x):
    return x.astype(MXU_DTYPE)


_NT = (((1,), (1,)), ((), ()))
_NN = (((1,), (0,)), ((), ()))
LOG2E = math.log2(math.e)


def _adaln_kernel(c_ref, w_ref, b_ref, o_ref):
    c = c_ref[...]
    act = _mx(c * jax.nn.sigmoid(c))
    o_ref[0] = _dot(act, _mx(w_ref[0])) + b_ref[0]


def _adaln(cvec, ada_w, ada_b):
    depth, d, n = ada_w.shape
    rows = cvec.shape[0]
    tn = 1024
    return pl.pallas_call(
        _adaln_kernel,
        grid=(depth, n // tn),
        in_specs=[
            pl.BlockSpec((rows, d), lambda i, j: (0, 0)),
            pl.BlockSpec((1, d, tn), lambda i, j: (i, 0, j)),
            pl.BlockSpec((1, 1, tn), lambda i, j: (i, 0, j)),
        ],
        out_specs=pl.BlockSpec((1, rows, tn), lambda i, j: (i, 0, j)),
        out_shape=jax.ShapeDtypeStruct((depth, rows, n), F32),
        compiler_params=_cparams(2),
        name="adaln",
    )(cvec, ada_w, ada_b.reshape(depth, 1, n))


def _ffn_kernel(h_ref, mod_ref, wg_ref, wu_ref, wd_ref, *rest, base, tf, mixer, final):
    rest = list(rest)
    o_ref = rest.pop()
    x = h_ref[0]
    if mixer:
        y_ref, wo_ref = rest[0], rest[1]
        x = x + mod_ref[0, 5:6, :] * _dot(y_ref[0], wo_ref[...])
    xn = _mx(_modulate(x, mod_ref, base))
    f = wg_ref.shape[1]
    acc = jnp.zeros(x.shape, F32)
    for c in range(f // tf):
        g = _dot(xn, wg_ref[:, c * tf:(c + 1) * tf])
        u = _dot(xn, wu_ref[:, c * tf:(c + 1) * tf])
        a = _mx(g * jax.nn.sigmoid(g) * u)
        acc = acc + _dot(a, wd_ref[c * tf:(c + 1) * tf, :])
    gate = mod_ref[0, base + 2:base + 3, :]
    out = x + (0.5 * gate) * acc
    if final:
        out = _rms(out) * rest[-1][...]
    o_ref[0] = out


def _ffn(h, mods, wg, wu, wd, base, mixer=None, final_gain=None, tm=512, tf=256):
    b, t, d = h.shape
    f = wg.shape[1]
    tm = min(tm, t)
    tok = pl.BlockSpec((1, tm, d), lambda i, j: (i, j, 0))
    in_specs = [tok, _mod_spec(mods, d), _resident((d, f)), _resident((d, f)), _resident((f, d))]
    args = [h, mods, wg, wu, wd]
    if mixer is not None:
        in_specs += [tok, _resident(mixer[1].shape)]
        args += list(mixer)
    if final_gain is not None:
        in_specs.append(_resident((1, d)))
        args.append(final_gain.reshape(1, d))
    return pl.pallas_call(
        functools.partial(_ffn_kernel, base=base, tf=tf, mixer=mixer is not None, final=final_gain is not None),
        grid=(b, t // tm),
        in_specs=in_specs,
        out_specs=tok,
        out_shape=jax.ShapeDtypeStruct((b, t, d), F32),
        compiler_params=_cparams(2),
        name="ffn",
    )(*args)


def _rope_tables(n_tokens, head_dim):
    rows = n_tokens // GRID_W
    row = jnp.repeat(jnp.arange(rows, dtype=jnp.int32), GRID_W).astype(F32)
    col = jnp.tile(jnp.arange(GRID_W, dtype=jnp.int32), rows).astype(F32)
    n_freq = head_dim // 4
    inv = jnp.power(ROPE_THETA, -jnp.arange(n_freq, dtype=F32) / n_freq)
    ang = jnp.concatenate([row[:, None] * inv, col[:, None] * inv], axis=-1)
    cos, sin = jnp.cos(ang), jnp.sin(ang)
    reps = LANES // head_dim
    cos_t = jnp.tile(jnp.concatenate([cos, cos], axis=-1), (1, reps))
    sin_t = jnp.tile(jnp.concatenate([-sin, sin], axis=-1), (1, reps))
    return cos_t, sin_t


def _rope_slab(x, cos, sin, half):
    if 2 * half == LANES:
        swapped = pltpu.roll(x, half, 1)
    else:
        lane = lax.broadcasted_iota(jnp.int32, x.shape, 1)
        first = (lane & (2 * half - 1)) < half
        swapped = jnp.where(first, pltpu.roll(x, LANES - half, 1), pltpu.roll(x, half, 1))
    return x * cos + swapped * sin


def _da_proj_kernel(h_ref, mod_ref, w_ref, *rest, rope):
    if rope:
        cos_ref, sin_ref, q_ref, k_ref, v_ref = rest
        cos, sin = cos_ref[...], sin_ref[...]
    else:
        q_ref, k_ref, v_ref = rest
    d = h_ref.shape[2]
    u = _mx(_modulate(h_ref[0], mod_ref, 3))
    scale = DA_HEAD_DIM ** -0.5 * LOG2E
    for idx, (o_ref, rot, mul) in enumerate(((q_ref, True, scale), (k_ref, True, 1.0), (v_ref, False, 1.0))):
        y = _dot(u, w_ref[:, idx * d:(idx + 1) * d])
        for j in range(d // LANES):
            ys = y[:, j * LANES:(j + 1) * LANES]
            if rot and rope:
                ys = _rope_slab(ys, cos, sin, DA_HEAD_DIM // 2)
            if mul != 1.0:
                ys = ys * mul
            o_ref[0, :, j * LANES:(j + 1) * LANES] = ys.astype(o_ref.dtype)


def _da_proj(h, mods, wqkv, tables, tm=512):
    b, t, d = h.shape
    tm = min(tm, t)
    rope = tables is not None
    tok = pl.BlockSpec((1, tm, d), lambda i, j: (i, j, 0))
    in_specs = [tok, _mod_spec(mods, d), _resident(wqkv.shape)]
    args = [h, mods, wqkv]
    if rope:
        tab = pl.BlockSpec((tm, LANES), lambda i, j: (j, 0))
        in_specs += [tab, tab]
        args += list(tables)
    out = jax.ShapeDtypeStruct((b, t, d), MXU_DTYPE)
    return pl.pallas_call(
        functools.partial(_da_proj_kernel, rope=rope),
        grid=(b, t // tm),
        in_specs=in_specs,
        out_specs=[tok, tok, tok],
        out_shape=[out, out, out],
        compiler_params=_cparams(2),
        name="da_proj",
    )(*args)


def _gq_proj_kernel(h_ref, mod_ref, w_ref, g_ref, *rest, rope):
    if rope:
        cos_ref, sin_ref, q_ref, k_ref, v_ref = rest
        cos, sin = cos_ref[...], sin_ref[...]
    else:
        q_ref, k_ref, v_ref = rest
    u = _mx(_modulate(h_ref[0], mod_ref, 3))
    nq = GQ_HEADS * GQ_HEAD_DIM
    nk = GQ_KV_HEADS * GQ_HEAD_DIM
    scale = GQ_HEAD_DIM ** -0.5 * LOG2E
    plan = ((q_ref, 0, nq, 0, scale), (k_ref, nq, nk, 1, 1.0), (v_ref, nq + nk, nk, None, 1.0))
    for o_ref, start, width, gain_row, mul in plan:
        y = _dot(u, w_ref[:, start:start + width])
        for j in range(width // LANES):
            ys = y[:, j * LANES:(j + 1) * LANES]
            if gain_row is not None:
                ys = _rms(ys) * g_ref[gain_row:gain_row + 1, :]
                if rope:
                    ys = _rope_slab(ys, cos, sin, GQ_HEAD_DIM // 2)
            if mul != 1.0:
                ys = ys * mul
            o_ref[0, :, j * LANES:(j + 1) * LANES] = ys.astype(o_ref.dtype)


def _gq_proj(h, mods, wqkv, qk_g, tables, tm=512):
    b, t, d = h.shape
    tm = min(tm, t)
    rope = tables is not None
    nq = GQ_HEADS * GQ_HEAD_DIM
    nk = GQ_KV_HEADS * GQ_HEAD_DIM
    tok = pl.BlockSpec((1, tm, d), lambda i, j: (i, j, 0))
    in_specs = [tok, _mod_spec(mods, d), _resident(wqkv.shape), _resident(qk_g.shape)]
    args = [h, mods, wqkv, qk_g]
    if rope:
        tab = pl.BlockSpec((tm, LANES), lambda i, j: (j, 0))
        in_specs += [tab, tab]
        args += list(tables)
    return pl.pallas_call(
        functools.partial(_gq_proj_kernel, rope=rope),
        grid=(b, t // tm),
        in_specs=in_specs,
        out_specs=[pl.BlockSpec((1, tm, nq), lambda i, j: (i, j, 0)),
                   pl.BlockSpec((1, tm, nk), lambda i, j: (i, j, 0)),
                   pl.BlockSpec((1, tm, nk), lambda i, j: (i, j, 0))],
        out_shape=[jax.ShapeDtypeStruct((b, t, nq), MXU_DTYPE),
                   jax.ShapeDtypeStruct((b, t, nk), MXU_DTYPE),
                   jax.ShapeDtypeStruct((b, t, nk), MXU_DTYPE)],
        compiler_params=_cparams(2),
        name="gq_proj",
    )(*args)


def _attn_pipeline(n_blocks, k_refs, s_scr, make_qs, finish):
    def scores(i):
        qs = make_qs(i)
        off = 0
        for k_ref in k_refs:
            n = k_ref.shape[1]
            s_scr[i % 2, :, off:off + n] = lax.dot_general(qs, k_ref[0], _NT, preferred_element_type=F32)
            off += n

    scores(0)
    for i in range(n_blocks):
        if i + 1 < n_blocks:
            scores(i + 1)
        s = s_scr[i % 2]
        p = jnp.exp2(s - jnp.max(s, axis=-1, keepdims=True))
        finish(i, p, jnp.sum(p, axis=-1, keepdims=True))


def _pv(a, v_refs):
    out, off = None, 0
    for v_ref in v_refs:
        n = v_ref.shape[1]
        part = _dot(a[:, off:off + n], v_ref[0])
        out = part if out is None else out + part
        off += n
    return out


def _da_attn_kernel(lam_ref, g_ref, q_ref, *rest, nseg, rb, lam_init):
    k_refs = [rest[2 * i] for i in range(nseg)]
    v_refs = [rest[2 * i + 1] for i in range(nseg)]
    o_ref, s_scr = rest[2 * nseg:]
    lv = lam_ref[...]
    lam = (jnp.exp(jnp.sum(lv[0:1] * lv[1:2], keepdims=True))
           - jnp.exp(jnp.sum(lv[2:3] * lv[3:4], keepdims=True)) + lam_init)
    gain = g_ref[...] * (1.0 - lam_init)
    lane = lax.broadcasted_iota(jnp.int32, (rb, LANES), 1)

    def make_qs(i):
        q = q_ref[0, i * rb:(i + 1) * rb, :]
        zero = jnp.zeros_like(q)
        return jnp.concatenate([jnp.where(lane < DA_HEAD_DIM, q, zero), jnp.where(lane >= DA_HEAD_DIM, q, zero)], axis=0)

    def finish(i, p, l):
        o = _pv(_mx(p), v_refs) / l
        y = _rms(o[:rb] - lam * o[rb:]) * gain
        o_ref[0, i * rb:(i + 1) * rb, :] = y.astype(o_ref.dtype)

    _attn_pipeline(q_ref.shape[1] // rb, k_refs, s_scr, make_qs, finish)


def _da_attn(q, kv_segs, lam_vecs, subln, lam_init, tq=1024, rb=128):
    b, t, d = q.shape
    tq = min(tq, t)
    rb = min(rb, tq)
    in_specs = [_resident(lam_vecs.shape), _resident(subln.shape),
                pl.BlockSpec((1, tq, LANES), lambda i, h, j: (i, j, h))]
    args = [lam_vecs, subln, q]
    n_keys = 0
    for k, v in kv_segs:
        n = k.shape[1]
        n_keys += n
        seg = pl.BlockSpec((1, n, LANES), lambda i, h, j: (i, 0, h))
        in_specs += [seg, seg]
        args += [k, v]
    return pl.pallas_call(
        functools.partial(_da_attn_kernel, nseg=len(kv_segs), rb=rb, lam_init=lam_init),
        grid=(b, DA_HEADS, t // tq),
        in_specs=in_specs,
        out_specs=pl.BlockSpec((1, tq, LANES), lambda i, h, j: (i, j, h)),
        out_shape=jax.ShapeDtypeStruct((b, t, d), MXU_DTYPE),
        scratch_shapes=[pltpu.VMEM((2, 2 * rb, n_keys), F32)],
        compiler_params=_cparams(3),
        name="da_attn",
    )(*args)


def _gq_attn_kernel(q_ref, *rest, nseg, rb):
    k_refs = [rest[2 * i] for i in range(nseg)]
    v_refs = [rest[2 * i + 1] for i in range(nseg)]
    o_ref, s_scr = rest[2 * nseg:]

    def make_qs(i):
        return jnp.concatenate([q_ref[0, i * rb:(i + 1) * rb, m * LANES:(m + 1) * LANES] for m in range(GQ_GROUP)], axis=0)

    def finish(i, p, l):
        o = _pv(_mx(p), v_refs) / l
        for m in range(GQ_GROUP):
            o_ref[0, i * rb:(i + 1) * rb, m * LANES:(m + 1) * LANES] = o[m * rb:(m + 1) * rb].astype(o_ref.dtype)

    _attn_pipeline(q_ref.shape[1] // rb, k_refs, s_scr, make_qs, finish)


def _gq_attn(q, kv_segs, tq=512, rb=32):
    b, t, d = q.shape
    tq = min(tq, t)
    rb = min(rb, tq)
    gw = GQ_GROUP * GQ_HEAD_DIM
    in_specs = [pl.BlockSpec((1, tq, gw), lambda i, g, j: (i, j, g))]
    args = [q]
    n_keys = 0
    for k, v in kv_segs:
        n = k.shape[1]
        n_keys += n
        seg = pl.BlockSpec((1, n, LANES), lambda i, g, j: (i, 0, g))
        in_specs += [seg, seg]
        args += [k, v]
    return pl.pallas_call(
        functools.partial(_gq_attn_kernel, nseg=len(kv_segs), rb=rb),
        grid=(b, GQ_KV_HEADS, t // tq),
        in_specs=in_specs,
        out_specs=pl.BlockSpec((1, tq, gw), lambda i, g, j: (i, j, g)),
        out_shape=jax.ShapeDtypeStruct((b, t, d), MXU_DTYPE),
        scratch_shapes=[pltpu.VMEM((2, GQ_GROUP * rb, n_keys), F32)],
        compiler_params=_cparams(3),
        name="gq_attn",
    )(*args)


def _head_sum(x, e_ref):
    return _dot(_mx(x), e_ref[...])


def _rw_tok_kernel(h_ref, hp_ref, hn_ref, mod_ref, mu_ref, wr_ref, wk_ref, wv_ref,
                   w0_ref, w1_ref, w2_ref, a0_ref, a1_ref, a2_ref, g1_ref, g2_ref, kk_ref, ka_ref, e_ref,
                   r_out, v_out, lw_out, kd_out, kn_out, ba_out, g_out):
    t = pl.program_id(1)
    nt = pl.num_programs(1)
    tm = h_ref.shape[1]
    z = _modulate(h_ref[0], mod_ref, 3)
    zp = _modulate(hp_ref[0], mod_ref, 3)[SUBLANES - 1:SUBLANES]
    zn = _modulate(hn_ref[0], mod_ref, 3)[0:1]
    zp = jnp.where(t > 0, zp, jnp.zeros_like(zp))
    zn = jnp.where(t < nt - 1, zn, jnp.zeros_like(zn))
    row = lax.broadcasted_iota(jnp.int32, z.shape, 0)
    dp = jnp.where(row == 0, zp, pltpu.roll(z, 1, 0)) - z
    dn = jnp.where(row == tm - 1, zn, pltpu.roll(z, tm - 1, 0)) - z

    def mix(i):
        return z + dp * mu_ref[0, i:i + 1, :] + dn * mu_ref[1, i:i + 1, :]

    r = _dot(_mx(mix(0)), wr_ref[...])
    k = _dot(_mx(mix(2)), wk_ref[...])
    v = _dot(_mx(mix(3)), wv_ref[...])
    xw, xa, xg = _mx(mix(1)), _mx(mix(4)), _mx(mix(5))
    r_out[0] = r
    v_out[0] = v
    for d in range(2):
        x = -(w0_ref[d:d + 1, :] + _dot(_mx(jnp.tanh(_dot(xw, w1_ref[d]))), w2_ref[d]))
        softplus = jnp.maximum(x, 0.0) + jnp.log(1.0 + jnp.exp(-jnp.abs(x)))
        lw_out[d, 0] = -jnp.exp(-softplus - 0.5)
        a = jax.nn.sigmoid(a0_ref[d:d + 1, :] + _dot(_mx(_dot(xa, a1_ref[d])), a2_ref[d]))
        g_out[d, 0] = _dot(_mx(jax.nn.sigmoid(_dot(xg, g1_ref[d]))), g2_ref[d])
        kx = k * kk_ref[d:d + 1, :]
        kn = kx / jnp.maximum(jnp.sqrt(_head_sum(kx * kx, e_ref)), 1e-12)
        kn_out[d, 0] = kn
        ba_out[d, 0] = kn * a
        kd_out[d, 0] = k * (1.0 + (a - 1.0) * ka_ref[d:d + 1, :])


def _rw_tok(h, mods, p, tm=128):
    b, t, d = h.shape
    tm = min(tm, t)
    nb8 = t // SUBLANES
    per8 = tm // SUBLANES
    tok = pl.BlockSpec((1, tm, d), lambda i, j: (i, j, 0))
    prev = pl.BlockSpec((1, SUBLANES, d), lambda i, j: (i, jnp.maximum(j * per8 - 1, 0), 0))
    nxt = pl.BlockSpec((1, SUBLANES, d), lambda i, j: (i, jnp.minimum((j + 1) * per8, nb8 - 1), 0))
    names = ("mu", "wr", "wk", "wv", "w0", "w1", "w2", "a0", "a1", "a2", "g1", "g2", "kk", "ka", "e")
    weights = [p[n] for n in names]
    both = pl.BlockSpec((2, 1, tm, d), lambda i, j: (0, i, j, 0))
    one = jax.ShapeDtypeStruct((b, t, d), F32)
    two = jax.ShapeDtypeStruct((2, b, t, d), F32)
    return pl.pallas_call(
        _rw_tok_kernel,
        grid=(b, t // tm),
        in_specs=[tok, prev, nxt, _mod_spec(mods, d)] + [_resident(w.shape) for w in weights],
        out_specs=[tok, tok, both, both, both, both, both],
        out_shape=[one, one, two, two, two, two, two],
        compiler_params=_cparams(2),
        name="rw_tok",
    )(h, h, h, mods, *weights)


def _mm(a, b, dims=_NN):
    return lax.dot_general(_mx(a), _mx(b), dims, preferred_element_type=F32)


def _split(x):
    hi = _mx(x)
    return hi, _mx(x - hi.astype(F32))


def _mm_state(s, b):
    sh, sl = _split(s)
    b = _mx(b)
    return _dot(sh, b) + _dot(sl, b)


def _cumsum_mm(tri, x):
    h1, rest = _mx(x), x - _mx(x).astype(F32)
    h2, h3 = _split(rest)
    dg = functools.partial(lax.dot_general, dimension_numbers=_NN, preferred_element_type=F32)
    return dg(tri, h1) + (dg(tri, h2) + dg(tri, h3))


def _scan_masks(rev):
    L = SCAN_CHUNK
    r1 = lax.broadcasted_iota(jnp.int32, (L, LANES), 0)
    c1 = lax.broadcasted_iota(jnp.int32, (L, LANES), 1) & (L - 1)
    tr = lax.broadcasted_iota(jnp.int32, (L, L), 0)
    tc = lax.broadcasted_iota(jnp.int32, (L, L), 1)
    rt6 = lax.broadcasted_iota(jnp.int32, (2 * L, LANES), 0) & (L - 1)
    ct6 = lax.broadcasted_iota(jnp.int32, (2 * L, LANES), 1) & (L - 1)
    late, early = (ct6, rt6) if rev else (rt6, ct6)
    off = [((rt6 >> (i + 1)) == (ct6 >> (i + 1))) & ((late & (1 << i)) != 0) & ((early & (1 << i)) == 0)
           for i in range(int(math.log2(L)))]
    return dict(
        incl=(c1 >= r1) if rev else (c1 <= r1),
        strict=(c1 > r1) if rev else (c1 < r1),
        tri=jnp.where((tc >= tr) if rev else (tc <= tr), 1.0, 0.0).astype(MXU_DTYPE),
        off=off,
    )


def _scan_kernel(s0_ref, *refs):
    L = SCAN_CHUNK
    in_refs = (refs[0:6], refs[6:12])
    y_refs = refs[12:14]
    sf_ref, s_scr = refs[14:]
    n_pairs = s_scr.shape[1]
    n_chunks = in_refs[0][0].shape[2] // L

    @pl.when(pl.program_id(1) == 0)
    def _():
        s_scr[...] = s0_ref[:, 0]

    masks = (_scan_masks(False), _scan_masks(True))
    r2 = lax.broadcasted_iota(jnp.int32, (2 * L, LANES), 0)
    c2 = lax.broadcasted_iota(jnp.int32, (2 * L, LANES), 1)
    own = (r2 < L) == (c2 < L)
    eye = jnp.where(r2 == c2, 1.0, 0.0).astype(F32)
    chains = [(d, p) for d in range(2) for p in range(n_pairs)]
    mk = [masks[d] for d, _ in chains]

    def bd(x):
        return jnp.where(own, jnp.concatenate([x, x], axis=0), 0.0)

    def each(fn, *lists):
        return [fn(*args) for args in zip(*lists)]

    def chunk(ci, carry):
        rows = (pl.ds(pl.multiple_of(ci * L, L), L), pl.ds(pl.multiple_of((n_chunks - 1 - ci) * L, L), L))

        def load(which, lead):
            out = []
            for d, p in chains:
                ref = in_refs[d][which]
                idx = (0, 0) if lead else (0,)
                out.append(ref[idx + (rows[d], slice(p * LANES, (p + 1) * LANES))])
            return out

        lw = load(0, True)
        cin = each(lambda m, x: _cumsum_mm(m["tri"], x), mk, lw)
        e_in = each(jnp.exp, cin)
        e_neg = each(lambda c: jnp.exp(-c), cin)
        at = each(lambda k_, c, x: -k_ * jnp.exp(c - x), load(1, True), cin, lw)
        rt = each(lambda r_, e: r_ * e, load(4, False), e_in)
        bt = each(lambda b_, e: b_ * e, load(2, True), e_neg)
        kt = each(lambda k_, e: k_ * e, load(3, True), e_neg)
        wl = [e[0:1] if d else e[L - 1:L] for (d, _), e in zip(chains, e_in)]
        g = each(lambda a, r_, b_, k_: _mm(jnp.concatenate([a, r_], axis=0),
                                           jnp.concatenate([bd(b_), bd(k_)], axis=0), _NT), at, rt, bt, kt)
        n_bd = each(lambda m, x: bd(jnp.where(m["strict"], x[:L, :LANES], 0.0)), mk, g)
        arb = each(lambda m, x: bd(jnp.where(m["incl"], x[L:, :LANES], 0.0)), mk, g)
        aak = each(lambda m, x: bd(jnp.where(m["strict"], x[:L, LANES:], 0.0)), mk, g)
        ark = each(lambda m, x: bd(jnp.where(m["incl"], x[L:, LANES:], 0.0)), mk, g)
        tinv = each(lambda m, n: eye + jnp.where(m["off"][0], n, 0.0), mk, n_bd)
        for lvl in range(1, int(math.log2(L))):
            inner = each(lambda m, n, t_: _mm(jnp.where(m["off"][lvl], n, 0.0), t_), mk, n_bd, tinv)
            tinv = each(lambda t_, x: t_ + _mm(t_, x), tinv, inner)
        v_sd = each(bd, load(5, False))
        akv = each(_mm, aak, v_sd)
        pq = each(lambda t_, a, x: _mm(t_, jnp.concatenate([bd(a), x], axis=1)), tinv, at, akv)
        bh = each(lambda b_, w: bd(b_ * w), bt, wl)
        kh = each(lambda k_, w: bd(k_ * w), kt, wl)
        qv = each(lambda x, v_: jnp.concatenate([x[:, LANES:], v_], axis=0), pq, v_sd)
        ptb = each(lambda x, b_: _mm(x[:, :LANES].T, b_), pq, bh)
        c_mat = each(lambda x, b_, k_: _mm(x.T, jnp.concatenate([b_, k_], axis=0)), qv, bh, kh)
        ry = each(lambda r_, a, x: bd(r_) + _mm(a, x[:, :LANES]), rt, arb, pq)
        y0 = each(lambda a, k_, x: _mm(jnp.concatenate([a, k_], axis=1), x), arb, ark, qv)
        s_in = [s_scr[d, p] for d, p in chains]
        ysd = each(lambda r_, s_, y_: _mm(r_, s_, _NT) + y_, ry, s_in, y0)
        for (d, p), y_ in zip(chains, ysd):
            y_refs[d][0, rows[d], p * LANES:(p + 1) * LANES] = y_[:L] + y_[L:]
        s_out = each(lambda s_, w, x, c: s_ * w + _mm_state(s_, x) + c, s_in, wl, ptb, c_mat)
        for (d, p), s_ in zip(chains, s_out):
            s_scr[d, p] = s_
        return carry

    lax.fori_loop(0, n_chunks, chunk, 0)
    sf_ref[:, 0] = s_scr[...]


def _scan(s0, lw, kn, ba, kd, r, v, tb=256):
    b, t, d = r.shape
    tb = min(tb, t)
    nb = t // tb
    n_pairs = d // LANES
    state = pl.BlockSpec((2, 1, n_pairs, LANES, LANES), lambda i, j: (0, i, 0, 0, 0))
    in_specs = [state]
    args = [s0]
    out_specs = []
    for direction in range(2):
        tmap = (lambda j: nb - 1 - j) if direction else (lambda j: j)
        dirspec = pl.BlockSpec((1, 1, tb, d), lambda i, j, direction=direction, tmap=tmap: (direction, i, tmap(j), 0))
        tok = pl.BlockSpec((1, tb, d), lambda i, j, tmap=tmap: (i, tmap(j), 0))
        in_specs += [dirspec, dirspec, dirspec, dirspec, tok, tok]
        args += [lw, kn, ba, kd, r, v]
        out_specs.append(tok)
    out_specs.append(state)
    y = jax.ShapeDtypeStruct((b, t, d), F32)
    return pl.pallas_call(
        _scan_kernel,
        grid=(b, nb),
        in_specs=in_specs,
        out_specs=out_specs,
        out_shape=[y, y, jax.ShapeDtypeStruct((2, b, n_pairs, LANES, LANES), F32)],
        scratch_shapes=[pltpu.VMEM((2, n_pairs, LANES, LANES), F32)],
        compiler_params=_cparams(2),
        name="rw_scan",
    )(*args)


def _rw_out_kernel(h_ref, mod_ref, y0_ref, y1_ref, kd_ref, g_ref, r_ref, v_ref,
                   rk_ref, lng_ref, lnb_ref, e_ref, wo_ref, o_ref):
    r = r_ref[0]
    v = v_ref[0]
    inv_n = 1.0 / RW_HEAD
    total = jnp.zeros(r.shape, F32)
    for d, y_ref in enumerate((y0_ref, y1_ref)):
        y = y_ref[0]
        yc = y - _head_sum(y, e_ref) * inv_n
        var = _head_sum(yc * yc, e_ref) * inv_n
        yn = yc * lax.rsqrt(var + RW_GN_EPS)
        bonus = _head_sum(r * kd_ref[d, 0] * rk_ref[d:d + 1, :], e_ref) * v
        out = yn * lng_ref[d:d + 1, :] + lnb_ref[d:d + 1, :] + bonus
        total = total + out * g_ref[d, 0]
    gate = mod_ref[0, 5:6, :]
    o_ref[0] = h_ref[0] + gate * _dot(_mx(total), wo_ref[...])


def _rw_out(h, mods, y0, y1, kd, g, r, v, p, tm=256):
    b, t, d = h.shape
    tm = min(tm, t)
    tok = pl.BlockSpec((1, tm, d), lambda i, j: (i, j, 0))
    both = pl.BlockSpec((2, 1, tm, d), lambda i, j: (0, i, j, 0))
    weights = [p["rk"], p["ln_g"], p["ln_b"], p["e"], p["wo"]]
    return pl.pallas_call(
        _rw_out_kernel,
        grid=(b, t // tm),
        in_specs=[tok, _mod_spec(mods, d), tok, tok, both, both, tok, tok] + [_resident(w.shape) for w in weights],
        out_specs=tok,
        out_shape=jax.ShapeDtypeStruct((b, t, d), F32),
        compiler_params=_cparams(2),
        name="rw_out",
    )(h, mods, y0, y1, kd, g, r, v, *weights)


def _da_layer(h, hc, mod, modc, wqkv, wo, lam_vecs, subln, lam_init, need_ctx):
    s = h.shape[1]
    wqkv, wo = _mx(wqkv), _mx(wo)
    subln = subln.reshape(1, -1)
    q, k, v = _da_proj(h, mod, wqkv, _rope_tables(s, DA_HEAD_DIM))
    qc, kc, vc = _da_proj(hc, modc, wqkv, None)
    y = _da_attn(q, [(kc, vc), (k, v)], lam_vecs, subln, lam_init)
    yc = _da_attn(qc, [(kc, vc)], lam_vecs, subln, lam_init) if need_ctx else None
    return (y, wo), (yc, wo)


def _gq_layer(h, hc, mod, modc, wqkv, wo, qk_g, need_ctx):
    s = h.shape[1]
    wqkv, wo = _mx(wqkv), _mx(wo)
    q, k, v = _gq_proj(h, mod, wqkv, qk_g, _rope_tables(s, GQ_HEAD_DIM))
    qc, kc, vc = _gq_proj(hc, modc, wqkv, qk_g, None)
    y = _gq_attn(q, [(kc, vc), (k, v)])
    yc = _gq_attn(qc, [(kc, vc)]) if need_ctx else None
    return (y, wo), (yc, wo)


def _rw_layer(h, hc, mod, modc, p, need_ctx):
    b, _, d = h.shape
    group = lax.broadcasted_iota(jnp.int32, (d, d), 0) // RW_HEAD == lax.broadcasted_iota(jnp.int32, (d, d), 1) // RW_HEAD
    p = dict(p, e=group.astype(MXU_DTYPE))
    r, v, lw, kd, kn, ba, g = _rw_tok(h, mod, p)
    rc, vc, lwc, kdc, knc, bac, gc = _rw_tok(hc, modc, p)
    zero = jnp.zeros((2, b, d // LANES, LANES, LANES), F32)
    yc0, yc1, state_c = _scan(zero, lwc, knc, bac, kdc, rc, vc)
    y0, y1, _ = _scan(state_c, lw, kn, ba, kd, r, v)
    h = _rw_out(h, mod, y0, y1, kd, g, r, v, p)
    if need_ctx:
        hc = _rw_out(hc, modc, yc0, yc1, kdc, gc, rc, vc, p)
    return h, hc


def kernel(x, c, ctx, c_ctx, ada_w, ada_b, ffn_wg, ffn_wu, ffn_wd, final_g, da_wqkv, da_wo, da_lam, da_subln, gq_wqkv, gq_wo, gq_qk_g, rw_mu, rw_wr, rw_wk, rw_wv, rw_wo, rw_w0, rw_w1, rw_w2, rw_a0, rw_a1, rw_a2, rw_g1, rw_g2, rw_kk, rw_ka, rw_rk, rw_ln_g, rw_ln_b):
    b, _, d = x.shape
    depth = ada_w.shape[0]
    rows = -(-(b + 1) // SUBLANES) * SUBLANES
    cvec = jnp.zeros((rows, d), F32).at[:b].set(c).at[b].set(c_ctx)
    mods = _adaln(cvec, ada_w, ada_b)
    h, hc = x, ctx
    for i in range(depth):
        kind, j = i % 3, i // 3
        need_ctx = i < depth - 1
        mod = mods[i, :b].reshape(b, N_MOD, d)
        modc = mods[i, b:b + 1].reshape(1, N_MOD, d)
        wg, wu, wd = _mx(ffn_wg[i, 0]), _mx(ffn_wu[i, 0]), _mx(ffn_wd[i, 0])
        h = _ffn(h, mod, wg, wu, wd, 0)
        hc = _ffn(hc, modc, wg, wu, wd, 0)
        mixer = mixer_c = None
        if kind == 0:
            lam_init = 0.8 - 0.6 * math.exp(-0.3 * i)
            mixer, mixer_c = _da_layer(h, hc, mod, modc, da_wqkv[j], da_wo[j], da_lam[j], da_subln[j], lam_init, need_ctx)
        elif kind == 1:
            mixer, mixer_c = _gq_layer(h, hc, mod, modc, gq_wqkv[j], gq_wo[j], gq_qk_g[j], need_ctx)
        else:
            p = dict(mu=rw_mu[j], wr=_mx(rw_wr[j]), wk=_mx(rw_wk[j]), wv=_mx(rw_wv[j]), wo=_mx(rw_wo[j]),
                     w0=rw_w0[j], w1=_mx(rw_w1[j]), w2=_mx(rw_w2[j]), a0=rw_a0[j], a1=_mx(rw_a1[j]),
                     a2=_mx(rw_a2[j]), g1=_mx(rw_g1[j]), g2=_mx(rw_g2[j]), kk=rw_kk[j], ka=rw_ka[j],
                     rk=rw_rk[j].reshape(2, d), ln_g=rw_ln_g[j], ln_b=rw_ln_b[j])
            h, hc = _rw_layer(h, hc, mod, modc, p, need_ctx)
        wg, wu, wd = _mx(ffn_wg[i, 1]), _mx(ffn_wu[i, 1]), _mx(ffn_wd[i, 1])
        h = _ffn(h, mod, wg, wu, wd, 6, mixer=mixer, final_gain=final_g if i == depth - 1 else None)
        if need_ctx:
            hc = _ffn(hc, modc, wg, wu, wd, 6, mixer=mixer_c)
    return h
```

```python
import functools
import math

import jax
import jax.numpy as jnp
from jax import lax
from jax.experimental import pallas as pl
from jax.experimental.pallas import tpu as pltpu

F32 = jnp.float32
MXU_DTYPE = jnp.bfloat16

NORM_EPS = 1e-6
ROPE_THETA = 10000.0
GRID_W = 64
N_MOD = 9
DA_HEADS = 8
DA_HEAD_DIM = 64
GQ_HEADS = 8
GQ_KV_HEADS = 2
GQ_HEAD_DIM = 128
GQ_GROUP = GQ_HEADS // GQ_KV_HEADS
RW_HEAD = 64
RW_GN_EPS = 64e-5

LANES = 128
SUBLANES = 8
VMEM_LIMIT_BYTES = 56 * 1024 * 1024

SCAN_CHUNK = 64


def _cparams(n_axes):
    return pltpu.CompilerParams(
        dimension_semantics=("arbitrary",) * n_axes,
        vmem_limit_bytes=VMEM_LIMIT_BYTES,
    )


def _resident(shape):
    zeros = (0,) * len(shape)
    return pl.BlockSpec(shape, lambda *_: zeros)


def _mod_spec(mods, d):
    if mods.shape[0] > 1:
        return pl.BlockSpec((1, N_MOD, d), lambda b, t: (b, 0, 0))
    return pl.BlockSpec((1, N_MOD, d), lambda b, t: (0, 0, 0))


def _rms(x):
    return x * lax.rsqrt(jnp.mean(x * x, axis=-1, keepdims=True) + NORM_EPS)


def _modulate(x, mod_ref, base):
    shift = mod_ref[0, base:base + 1, :]
    scale = mod_ref[0, base + 1:base + 2, :]
    return _rms(x) * (1.0 + scale) + shift


def _dot(a, b):
    return jnp.dot(a, b, preferred_element_type=F32)


def _mx(x):
    return x.astype(MXU_DTYPE)


_NT = (((1,), (1,)), ((), ()))
_NN = (((1,), (0,)), ((), ()))
LOG2E = math.log2(math.e)


def _adaln_kernel(c_ref, w_ref, b_ref, o_ref):
    c = c_ref[...]
    act = _mx(c * jax.nn.sigmoid(c))
    o_ref[0] = _dot(act, _mx(w_ref[0])) + b_ref[0]


def _adaln(cvec, ada_w, ada_b):
    depth, d, n = ada_w.shape
    rows = cvec.shape[0]
    tn = 1024
    return pl.pallas_call(
        _adaln_kernel,
        grid=(depth, n // tn),
        in_specs=[
            pl.BlockSpec((rows, d), lambda i, j: (0, 0)),
            pl.BlockSpec((1, d, tn), lambda i, j: (i, 0, j)),
            pl.BlockSpec((1, 1, tn), lambda i, j: (i, 0, j)),
        ],
        out_specs=pl.BlockSpec((1, rows, tn), lambda i, j: (i, 0, j)),
        out_shape=jax.ShapeDtypeStruct((depth, rows, n), F32),
        compiler_params=_cparams(2),
        name="adaln",
    )(cvec, ada_w, ada_b.reshape(depth, 1, n))


def _ffn_kernel(h_ref, mod_ref, wg_ref, wu_ref, wd_ref, *rest, base, tf, mixer, final):
    rest = list(rest)
    o_ref = rest.pop()
    x = h_ref[0]
    if mixer:
        y_ref, wo_ref = rest[0], rest[1]
        x = x + mod_ref[0, 5:6, :] * _dot(y_ref[0], wo_ref[...])
    xn = _mx(_modulate(x, mod_ref, base))
    f = wg_ref.shape[1]
    acc = jnp.zeros(x.shape, F32)
    for c in range(f // tf):
        g = _dot(xn, wg_ref[:, c * tf:(c + 1) * tf])
        u = _dot(xn, wu_ref[:, c * tf:(c + 1) * tf])
        a = _mx(g * jax.nn.sigmoid(g) * u)
        acc = acc + _dot(a, wd_ref[c * tf:(c + 1) * tf, :])
    gate = mod_ref[0, base + 2:base + 3, :]
    out = x + (0.5 * gate) * acc
    if final:
        out = _rms(out) * rest[-1][...]
    o_ref[0] = out


def _ffn(h, mods, wg, wu, wd, base, mixer=None, final_gain=None, tm=512, tf=256):
    b, t, d = h.shape
    f = wg.shape[1]
    tm = min(tm, t)
    tok = pl.BlockSpec((1, tm, d), lambda i, j: (i, j, 0))
    in_specs = [tok, _mod_spec(mods, d), _resident((d, f)), _resident((d, f)), _resident((f, d))]
    args = [h, mods, wg, wu, wd]
    if mixer is not None:
        in_specs += [tok, _resident(mixer[1].shape)]
        args += list(mixer)
    if final_gain is not None:
        in_specs.append(_resident((1, d)))
        args.append(final_gain.reshape(1, d))
    return pl.pallas_call(
        functools.partial(_ffn_kernel, base=base, tf=tf, mixer=mixer is not None, final=final_gain is not None),
        grid=(b, t // tm),
        in_specs=in_specs,
        out_specs=tok,
        out_shape=jax.ShapeDtypeStruct((b, t, d), F32),
        compiler_params=_cparams(2),
        name="ffn",
    )(*args)


def _rope_tables(n_tokens, head_dim):
    rows = n_tokens // GRID_W
    row = jnp.repeat(jnp.arange(rows, dtype=jnp.int32), GRID_W).astype(F32)
    col = jnp.tile(jnp.arange(GRID_W, dtype=jnp.int32), rows).astype(F32)
    n_freq = head_dim // 4
    inv = jnp.power(ROPE_THETA, -jnp.arange(n_freq, dtype=F32) / n_freq)
    ang = jnp.concatenate([row[:, None] * inv, col[:, None] * inv], axis=-1)
    cos, sin = jnp.cos(ang), jnp.sin(ang)
    reps = LANES // head_dim
    cos_t = jnp.tile(jnp.concatenate([cos, cos], axis=-1), (1, reps))
    sin_t = jnp.tile(jnp.concatenate([-sin, sin], axis=-1), (1, reps))
    return cos_t, sin_t


def _rope_slab(x, cos, sin, half):
    if 2 * half == LANES:
        swapped = pltpu.roll(x, half, 1)
    else:
        lane = lax.broadcasted_iota(jnp.int32, x.shape, 1)
        first = (lane & (2 * half - 1)) < half
        swapped = jnp.where(first, pltpu.roll(x, LANES - half, 1), pltpu.roll(x, half, 1))
    return x * cos + swapped * sin


def _da_proj_kernel(h_ref, mod_ref, w_ref, *rest, rope):
    if rope:
        cos_ref, sin_ref, q_ref, k_ref, v_ref = rest
        cos, sin = cos_ref[...], sin_ref[...]
    else:
        q_ref, k_ref, v_ref = rest
    d = h_ref.shape[2]
    u = _mx(_modulate(h_ref[0], mod_ref, 3))
    scale = DA_HEAD_DIM ** -0.5 * LOG2E
    for idx, (o_ref, rot, mul) in enumerate(((q_ref, True, scale), (k_ref, True, 1.0), (v_ref, False, 1.0))):
        y = _dot(u, w_ref[:, idx * d:(idx + 1) * d])
        for j in range(d // LANES):
            ys = y[:, j * LANES:(j + 1) * LANES]
            if rot and rope:
                ys = _rope_slab(ys, cos, sin, DA_HEAD_DIM // 2)
            if mul != 1.0:
                ys = ys * mul
            o_ref[0, :, j * LANES:(j + 1) * LANES] = ys.astype(o_ref.dtype)


def _da_proj(h, mods, wqkv, tables, tm=512):
    b, t, d = h.shape
    tm = min(tm, t)
    rope = tables is not None
    tok = pl.BlockSpec((1, tm, d), lambda i, j: (i, j, 0))
    in_specs = [tok, _mod_spec(mods, d), _resident(wqkv.shape)]
    args = [h, mods, wqkv]
    if rope:
        tab = pl.BlockSpec((tm, LANES), lambda i, j: (j, 0))
        in_specs += [tab, tab]
        args += list(tables)
    out = jax.ShapeDtypeStruct((b, t, d), MXU_DTYPE)
    return pl.pallas_call(
        functools.partial(_da_proj_kernel, rope=rope),
        grid=(b, t // tm),
        in_specs=in_specs,
        out_specs=[tok, tok, tok],
        out_shape=[out, out, out],
        compiler_params=_cparams(2),
        name="da_proj",
    )(*args)


def _gq_proj_kernel(h_ref, mod_ref, w_ref, g_ref, *rest, rope):
    if rope:
        cos_ref, sin_ref, q_ref, k_ref, v_ref = rest
        cos, sin = cos_ref[...], sin_ref[...]
    else:
        q_ref, k_ref, v_ref = rest
    u = _mx(_modulate(h_ref[0], mod_ref, 3))
    nq = GQ_HEADS * GQ_HEAD_DIM
    nk = GQ_KV_HEADS * GQ_HEAD_DIM
    scale = GQ_HEAD_DIM ** -0.5 * LOG2E
    plan = ((q_ref, 0, nq, 0, scale), (k_ref, nq, nk, 1, 1.0), (v_ref, nq + nk, nk, None, 1.0))
    for o_ref, start, width, gain_row, mul in plan:
        y = _dot(u, w_ref[:, start:start + width])
        for j in range(width // LANES):
            ys = y[:, j * LANES:(j + 1) * LANES]
            if gain_row is not None:
                ys = _rms(ys) * g_ref[gain_row:gain_row + 1, :]
                if rope:
                    ys = _rope_slab(ys, cos, sin, GQ_HEAD_DIM // 2)
            if mul != 1.0:
                ys = ys * mul
            o_ref[0, :, j * LANES:(j + 1) * LANES] = ys.astype(o_ref.dtype)


def _gq_proj(h, mods, wqkv, qk_g, tables, tm=512):
    b, t, d = h.shape
    tm = min(tm, t)
    rope = tables is not None
    nq = GQ_HEADS * GQ_HEAD_DIM
    nk = GQ_KV_HEADS * GQ_HEAD_DIM
    tok = pl.BlockSpec((1, tm, d), lambda i, j: (i, j, 0))
    in_specs = [tok, _mod_spec(mods, d), _resident(wqkv.shape), _resident(qk_g.shape)]
    args = [h, mods, wqkv, qk_g]
    if rope:
        tab = pl.BlockSpec((tm, LANES), lambda i, j: (j, 0))
        in_specs += [tab, tab]
        args += list(tables)
    return pl.pallas_call(
        functools.partial(_gq_proj_kernel, rope=rope),
        grid=(b, t // tm),
        in_specs=in_specs,
        out_specs=[pl.BlockSpec((1, tm, nq), lambda i, j: (i, j, 0)),
                   pl.BlockSpec((1, tm, nk), lambda i, j: (i, j, 0)),
                   pl.BlockSpec((1, tm, nk), lambda i, j: (i, j, 0))],
        out_shape=[jax.ShapeDtypeStruct((b, t, nq), MXU_DTYPE),
                   jax.ShapeDtypeStruct((b, t, nk), MXU_DTYPE),
                   jax.ShapeDtypeStruct((b, t, nk), MXU_DTYPE)],
        compiler_params=_cparams(2),
        name="gq_proj",
    )(*args)


def _attn_pipeline(n_blocks, k_refs, s_scr, make_qs, finish):
    def scores(i):
        qs = make_qs(i)
        off = 0
        for k_ref in k_refs:
            n = k_ref.shape[1]
            s_scr[i % 2, :, off:off + n] = lax.dot_general(qs, k_ref[0], _NT, preferred_element_type=F32)
            off += n

    scores(0)
    for i in range(n_blocks):
        if i + 1 < n_blocks:
            scores(i + 1)
        s = s_scr[i % 2]
        p = jnp.exp2(s - jnp.max(s, axis=-1, keepdims=True))
        finish(i, p, jnp.sum(p, axis=-1, keepdims=True))


def _pv(a, v_refs):
    out, off = None, 0
    for v_ref in v_refs:
        n = v_ref.shape[1]
        part = _dot(a[:, off:off + n], v_ref[0])
        out = part if out is None else out + part
        off += n
    return out


def _da_attn_kernel(lam_ref, g_ref, q_ref, *rest, nseg, rb, lam_init):
    k_refs = [rest[2 * i] for i in range(nseg)]
    v_refs = [rest[2 * i + 1] for i in range(nseg)]
    o_ref, s_scr = rest[2 * nseg:]
    lv = lam_ref[...]
    lam = (jnp.exp(jnp.sum(lv[0:1] * lv[1:2], keepdims=True))
           - jnp.exp(jnp.sum(lv[2:3] * lv[3:4], keepdims=True)) + lam_init)
    gain = g_ref[...] * (1.0 - lam_init)
    lane = lax.broadcasted_iota(jnp.int32, (rb, LANES), 1)

    def make_qs(i):
        q = q_ref[0, i * rb:(i + 1) * rb, :]
        zero = jnp.zeros_like(q)
        return jnp.concatenate([jnp.where(lane < DA_HEAD_DIM, q, zero), jnp.where(lane >= DA_HEAD_DIM, q, zero)], axis=0)

    def finish(i, p, l):
        o = _pv(_mx(p), v_refs) / l
        y = _rms(o[:rb] - lam * o[rb:]) * gain
        o_ref[0, i * rb:(i + 1) * rb, :] = y.astype(o_ref.dtype)

    _attn_pipeline(q_ref.shape[1] // rb, k_refs, s_scr, make_qs, finish)


def _da_attn(q, kv_segs, lam_vecs, subln, lam_init, tq=1024, rb=128):
    b, t, d = q.shape
    tq = min(tq, t)
    rb = min(rb, tq)
    in_specs = [_resident(lam_vecs.shape), _resident(subln.shape),
                pl.BlockSpec((1, tq, LANES), lambda i, h, j: (i, j, h))]
    args = [lam_vecs, subln, q]
    n_keys = 0
    for k, v in kv_segs:
        n = k.shape[1]
        n_keys += n
        seg = pl.BlockSpec((1, n, LANES), lambda i, h, j: (i, 0, h))
        in_specs += [seg, seg]
        args += [k, v]
    return pl.pallas_call(
        functools.partial(_da_attn_kernel, nseg=len(kv_segs), rb=rb, lam_init=lam_init),
        grid=(b, DA_HEADS, t // tq),
        in_specs=in_specs,
        out_specs=pl.BlockSpec((1, tq, LANES), lambda i, h, j: (i, j, h)),
        out_shape=jax.ShapeDtypeStruct((b, t, d), MXU_DTYPE),
        scratch_shapes=[pltpu.VMEM((2, 2 * rb, n_keys), F32)],
        compiler_params=_cparams(3),
        name="da_attn",
    )(*args)


def _gq_attn_kernel(q_ref, *rest, nseg, rb):
    k_refs = [rest[2 * i] for i in range(nseg)]
    v_refs = [rest[2 * i + 1] for i in range(nseg)]
    o_ref, s_scr = rest[2 * nseg:]

    def make_qs(i):
        return jnp.concatenate([q_ref[0, i * rb:(i + 1) * rb, m * LANES:(m + 1) * LANES] for m in range(GQ_GROUP)], axis=0)

    def finish(i, p, l):
        o = _pv(_mx(p), v_refs) / l
        for m in range(GQ_GROUP):
            o_ref[0, i * rb:(i + 1) * rb, m * LANES:(m + 1) * LANES] = o[m * rb:(m + 1) * rb].astype(o_ref.dtype)

    _attn_pipeline(q_ref.shape[1] // rb, k_refs, s_scr, make_qs, finish)


def _gq_attn(q, kv_segs, tq=512, rb=32):
    b, t, d = q.shape
    tq = min(tq, t)
    rb = min(rb, tq)
    gw = GQ_GROUP * GQ_HEAD_DIM
    in_specs = [pl.BlockSpec((1, tq, gw), lambda i, g, j: (i, j, g))]
    args = [q]
    n_keys = 0
    for k, v in kv_segs:
        n = k.shape[1]
        n_keys += n
        seg = pl.BlockSpec((1, n, LANES), lambda i, g, j: (i, 0, g))
        in_specs += [seg, seg]
        args += [k, v]
    return pl.pallas_call(
        functools.partial(_gq_attn_kernel, nseg=len(kv_segs), rb=rb),
        grid=(b, GQ_KV_HEADS, t // tq),
        in_specs=in_specs,
        out_specs=pl.BlockSpec((1, tq, gw), lambda i, g, j: (i, j, g)),
        out_shape=jax.ShapeDtypeStruct((b, t, d), MXU_DTYPE),
        scratch_shapes=[pltpu.VMEM((2, GQ_GROUP * rb, n_keys), F32)],
        compiler_params=_cparams(3),
        name="gq_attn",
    )(*args)


def _head_sum(x, e_ref):
    return _dot(_mx(x), e_ref[...])


def _rw_tok_kernel(h_ref, hp_ref, hn_ref, mod_ref, mu_ref, wr_ref, wk_ref, wv_ref,
                   w0_ref, w1_ref, w2_ref, a0_ref, a1_ref, a2_ref, g1_ref, g2_ref, kk_ref, ka_ref, e_ref,
                   r_out, v_out, lw_out, kd_out, kn_out, ba_out, g_out):
    t = pl.program_id(1)
    nt = pl.num_programs(1)
    tm = h_ref.shape[1]
    z = _modulate(h_ref[0], mod_ref, 3)
    zp = _modulate(hp_ref[0], mod_ref, 3)[SUBLANES - 1:SUBLANES]
    zn = _modulate(hn_ref[0], mod_ref, 3)[0:1]
    zp = jnp.where(t > 0, zp, jnp.zeros_like(zp))
    zn = jnp.where(t < nt - 1, zn, jnp.zeros_like(zn))
    row = lax.broadcasted_iota(jnp.int32, z.shape, 0)
    dp = jnp.where(row == 0, zp, pltpu.roll(z, 1, 0)) - z
    dn = jnp.where(row == tm - 1, zn, pltpu.roll(z, tm - 1, 0)) - z

    def mix(i):
        return z + dp * mu_ref[0, i:i + 1, :] + dn * mu_ref[1, i:i + 1, :]

    r = _dot(_mx(mix(0)), wr_ref[...])
    k = _dot(_mx(mix(2)), wk_ref[...])
    v = _dot(_mx(mix(3)), wv_ref[...])
    xw, xa, xg = _mx(mix(1)), _mx(mix(4)), _mx(mix(5))
    r_out[0] = r
    v_out[0] = v
    for d in range(2):
        x = -(w0_ref[d:d + 1, :] + _dot(_mx(jnp.tanh(_dot(xw, w1_ref[d]))), w2_ref[d]))
        softplus = jnp.maximum(x, 0.0) + jnp.log(1.0 + jnp.exp(-jnp.abs(x)))
        lw_out[d, 0] = -jnp.exp(-softplus - 0.5)
        a = jax.nn.sigmoid(a0_ref[d:d + 1, :] + _dot(_mx(_dot(xa, a1_ref[d])), a2_ref[d]))
        g_out[d, 0] = _dot(_mx(jax.nn.sigmoid(_dot(xg, g1_ref[d]))), g2_ref[d])
        kx = k * kk_ref[d:d + 1, :]
        kn = kx / jnp.maximum(jnp.sqrt(_head_sum(kx * kx, e_ref)), 1e-12)
        kn_out[d, 0] = kn
        ba_out[d, 0] = kn * a
        kd_out[d, 0] = k * (1.0 + (a - 1.0) * ka_ref[d:d + 1, :])


def _rw_tok(h, mods, p, tm=128):
    b, t, d = h.shape
    tm = min(tm, t)
    nb8 = t // SUBLANES
    per8 = tm // SUBLANES
    tok = pl.BlockSpec((1, tm, d), lambda i, j: (i, j, 0))
    prev = pl.BlockSpec((1, SUBLANES, d), lambda i, j: (i, jnp.maximum(j * per8 - 1, 0), 0))
    nxt = pl.BlockSpec((1, SUBLANES, d), lambda i, j: (i, jnp.minimum((j + 1) * per8, nb8 - 1), 0))
    names = ("mu", "wr", "wk", "wv", "w0", "w1", "w2", "a0", "a1", "a2", "g1", "g2", "kk", "ka", "e")
    weights = [p[n] for n in names]
    both = pl.BlockSpec((2, 1, tm, d), lambda i, j: (0, i, j, 0))
    one = jax.ShapeDtypeStruct((b, t, d), F32)
    two = jax.ShapeDtypeStruct((2, b, t, d), F32)
    return pl.pallas_call(
        _rw_tok_kernel,
        grid=(b, t // tm),
        in_specs=[tok, prev, nxt, _mod_spec(mods, d)] + [_resident(w.shape) for w in weights],
        out_specs=[tok, tok, both, both, both, both, both],
        out_shape=[one, one, two, two, two, two, two],
        compiler_params=_cparams(2),
        name="rw_tok",
    )(h, h, h, mods, *weights)


def _mm(a, b, dims=_NN):
    return lax.dot_general(_mx(a), _mx(b), dims, preferred_element_type=F32)


def _split(x):
    hi = _mx(x)
    return hi, _mx(x - hi.astype(F32))


def _cumsum_mm(tri, x):
    hi, lo = _split(x)
    return _dot(tri, hi) + _dot(tri, lo)


def _scan_masks(rev):
    L = SCAN_CHUNK
    r1 = lax.broadcasted_iota(jnp.int32, (L, LANES), 0)
    c1 = lax.broadcasted_iota(jnp.int32, (L, LANES), 1) & (L - 1)
    tr = lax.broadcasted_iota(jnp.int32, (L, L), 0)
    tc = lax.broadcasted_iota(jnp.int32, (L, L), 1)
    rt6 = lax.broadcasted_iota(jnp.int32, (2 * L, LANES), 0) & (L - 1)
    ct6 = lax.broadcasted_iota(jnp.int32, (2 * L, LANES), 1) & (L - 1)
    late, early = (ct6, rt6) if rev else (rt6, ct6)
    off = [((rt6 >> (i + 1)) == (ct6 >> (i + 1))) & ((late & (1 << i)) != 0) & ((early & (1 << i)) == 0)
           for i in range(int(math.log2(L)))]
    return dict(
        incl=(c1 >= r1) if rev else (c1 <= r1),
        strict=(c1 > r1) if rev else (c1 < r1),
        tri=jnp.where((tc >= tr) if rev else (tc <= tr), 1.0, 0.0).astype(MXU_DTYPE),
        off=off,
    )


def _scan_kernel(s0_ref, *refs):
    L = SCAN_CHUNK
    in_refs = (refs[0:6], refs[6:12])
    y_refs = refs[12:14]
    sf_ref, s_scr = refs[14:]
    n_pairs = s_scr.shape[1]
    n_chunks = in_refs[0][0].shape[2] // L

    @pl.when(pl.program_id(1) == 0)
    def _():
        s_scr[...] = s0_ref[:, 0]

    masks = (_scan_masks(False), _scan_masks(True))
    r2 = lax.broadcasted_iota(jnp.int32, (2 * L, LANES), 0)
    c2 = lax.broadcasted_iota(jnp.int32, (2 * L, LANES), 1)
    own = (r2 < L) == (c2 < L)
    eye = jnp.where(r2 == c2, 1.0, 0.0).astype(F32)
    chains = [(d, p) for d in range(2) for p in range(n_pairs)]
    mk = [masks[d] for d, _ in chains]

    def bd(x):
        return jnp.where(own, jnp.concatenate([x, x], axis=0), 0.0)

    def each(fn, *lists):
        return [fn(*args) for args in zip(*lists)]

    def chunk(ci, carry):
        rows = (pl.ds(pl.multiple_of(ci * L, L), L), pl.ds(pl.multiple_of((n_chunks - 1 - ci) * L, L), L))

        def load(which, lead):
            out = []
            for d, p in chains:
                ref = in_refs[d][which]
                idx = (0, 0) if lead else (0,)
                out.append(ref[idx + (rows[d], slice(p * LANES, (p + 1) * LANES))])
            return out

        lw = load(0, True)
        cin = each(lambda m, x: _cumsum_mm(m["tri"], x), mk, lw)
        e_in = each(jnp.exp, cin)
        e_neg = each(lambda c: jnp.exp(-c), cin)
        at = each(lambda k_, c, x: -k_ * jnp.exp(c - x), load(1, True), cin, lw)
        rt = each(lambda r_, e: r_ * e, load(4, False), e_in)
        bt = each(lambda b_, e: b_ * e, load(2, True), e_neg)
        kt = each(lambda k_, e: k_ * e, load(3, True), e_neg)
        wl = [e[0:1] if d else e[L - 1:L] for (d, _), e in zip(chains, e_in)]
        g = each(lambda a, r_, b_, k_: _mm(jnp.concatenate([a, r_], axis=0),
                                           jnp.concatenate([bd(b_), bd(k_)], axis=0), _NT), at, rt, bt, kt)
        n_bd = each(lambda m, x: bd(jnp.where(m["strict"], x[:L, :LANES], 0.0)), mk, g)
        arb = each(lambda m, x: bd(jnp.where(m["incl"], x[L:, :LANES], 0.0)), mk, g)
        aak = each(lambda m, x: bd(jnp.where(m["strict"], x[:L, LANES:], 0.0)), mk, g)
        ark = each(lambda m, x: bd(jnp.where(m["incl"], x[L:, LANES:], 0.0)), mk, g)
        tinv = each(lambda m, n: eye + jnp.where(m["off"][0], n, 0.0), mk, n_bd)
        for lvl in range(1, int(math.log2(L))):
            inner = each(lambda m, n, t_: _mm(jnp.where(m["off"][lvl], n, 0.0), t_), mk, n_bd, tinv)
            tinv = each(lambda t_, x: t_ + _mm(t_, x), tinv, inner)
        v_sd = each(bd, load(5, False))
        akv = each(_mm, aak, v_sd)
        pq = each(lambda t_, a, x: _mm(t_, jnp.concatenate([bd(a), x], axis=1)), tinv, at, akv)
        bh = each(lambda b_, w: bd(b_ * w), bt, wl)
        kh = each(lambda k_, w: bd(k_ * w), kt, wl)
        qv = each(lambda x, v_: jnp.concatenate([x[:, LANES:], v_], axis=0), pq, v_sd)
        ptb = each(lambda x, b_: _mm(x[:, :LANES].T, b_), pq, bh)
        c_mat = each(lambda x, b_, k_: _mm(x.T, jnp.concatenate([b_, k_], axis=0)), qv, bh, kh)
        ry = each(lambda r_, a, x: bd(r_) + _mm(a, x[:, :LANES]), rt, arb, pq)
        y0 = each(lambda a, k_, x: _mm(jnp.concatenate([a, k_], axis=1), x), arb, ark, qv)
        s_in = [s_scr[d, p] for d, p in chains]
        ysd = each(lambda r_, s_, y_: _mm(r_, s_, _NT) + y_, ry, s_in, y0)
        for (d, p), y_ in zip(chains, ysd):
            y_refs[d][0, rows[d], p * LANES:(p + 1) * LANES] = y_[:L] + y_[L:]
        s_out = each(lambda s_, w, x, c: s_ * w + _mm(s_, x) + c, s_in, wl, ptb, c_mat)
        for (d, p), s_ in zip(chains, s_out):
            s_scr[d, p] = s_
        return carry

    lax.fori_loop(0, n_chunks, chunk, 0)
    sf_ref[:, 0] = s_scr[...]


def _scan(s0, lw, kn, ba, kd, r, v, tb=256):
    b, t, d = r.shape
    tb = min(tb, t)
    nb = t // tb
    n_pairs = d // LANES
    state = pl.BlockSpec((2, 1, n_pairs, LANES, LANES), lambda i, j: (0, i, 0, 0, 0))
    in_specs = [state]
    args = [s0]
    out_specs = []
    for direction in range(2):
        tmap = (lambda j: nb - 1 - j) if direction else (lambda j: j)
        dirspec = pl.BlockSpec((1, 1, tb, d), lambda i, j, direction=direction, tmap=tmap: (direction, i, tmap(j), 0))
        tok = pl.BlockSpec((1, tb, d), lambda i, j, tmap=tmap: (i, tmap(j), 0))
        in_specs += [dirspec, dirspec, dirspec, dirspec, tok, tok]
        args += [lw, kn, ba, kd, r, v]
        out_specs.append(tok)
    out_specs.append(state)
    y = jax.ShapeDtypeStruct((b, t, d), F32)
    return pl.pallas_call(
        _scan_kernel,
        grid=(b, nb),
        in_specs=in_specs,
        out_specs=out_specs,
        out_shape=[y, y, jax.ShapeDtypeStruct((2, b, n_pairs, LANES, LANES), F32)],
        scratch_shapes=[pltpu.VMEM((2, n_pairs, LANES, LANES), F32)],
        compiler_params=_cparams(2),
        name="rw_scan",
    )(*args)


def _head_sum_2step(x, hd_ref, hu_ref):
    hi, lo = _split(_dot(_mx(x), hd_ref[...]))
    return _dot(hi, hu_ref[...]) + _dot(lo, hu_ref[...])


def _rw_out_kernel(h_ref, mod_ref, y0_ref, y1_ref, kd_ref, g_ref, r_ref, v_ref,
                   rk_ref, lng_ref, lnb_ref, hd_ref, hu_ref, wo_ref, o_ref):
    r = r_ref[0]
    v = v_ref[0]
    inv_n = 1.0 / RW_HEAD
    total = jnp.zeros(r.shape, F32)
    for d, y_ref in enumerate((y0_ref, y1_ref)):
        y = y_ref[0]
        yc = y - _head_sum_2step(y, hd_ref, hu_ref) * inv_n
        var = _head_sum_2step(yc * yc, hd_ref, hu_ref) * inv_n
        yn = yc * lax.rsqrt(var + RW_GN_EPS)
        bonus = _head_sum_2step(r * kd_ref[d, 0] * rk_ref[d:d + 1, :], hd_ref, hu_ref) * v
        out = yn * lng_ref[d:d + 1, :] + lnb_ref[d:d + 1, :] + bonus
        total = total + out * g_ref[d, 0]
    gate = mod_ref[0, 5:6, :]
    o_ref[0] = h_ref[0] + gate * _dot(_mx(total), wo_ref[...])


def _rw_out(h, mods, y0, y1, kd, g, r, v, p, tm=256):
    b, t, d = h.shape
    tm = min(tm, t)
    tok = pl.BlockSpec((1, tm, d), lambda i, j: (i, j, 0))
    both = pl.BlockSpec((2, 1, tm, d), lambda i, j: (0, i, j, 0))
    weights = [p["rk"], p["ln_g"], p["ln_b"], p["head_down"], p["head_up"], p["wo"]]
    return pl.pallas_call(
        _rw_out_kernel,
        grid=(b, t // tm),
        in_specs=[tok, _mod_spec(mods, d), tok, tok, both, both, tok, tok] + [_resident(w.shape) for w in weights],
        out_specs=tok,
        out_shape=jax.ShapeDtypeStruct((b, t, d), F32),
        compiler_params=_cparams(2),
        name="rw_out",
    )(h, mods, y0, y1, kd, g, r, v, *weights)


def _da_layer(h, hc, mod, modc, wqkv, wo, lam_vecs, subln, lam_init, need_ctx):
    s = h.shape[1]
    wqkv, wo = _mx(wqkv), _mx(wo)
    subln = subln.reshape(1, -1)
    q, k, v = _da_proj(h, mod, wqkv, _rope_tables(s, DA_HEAD_DIM))
    qc, kc, vc = _da_proj(hc, modc, wqkv, None)
    y = _da_attn(q, [(kc, vc), (k, v)], lam_vecs, subln, lam_init)
    yc = _da_attn(qc, [(kc, vc)], lam_vecs, subln, lam_init) if need_ctx else None
    return (y, wo), (yc, wo)


def _gq_layer(h, hc, mod, modc, wqkv, wo, qk_g, need_ctx):
    s = h.shape[1]
    wqkv, wo = _mx(wqkv), _mx(wo)
    q, k, v = _gq_proj(h, mod, wqkv, qk_g, _rope_tables(s, GQ_HEAD_DIM))
    qc, kc, vc = _gq_proj(hc, modc, wqkv, qk_g, None)
    y = _gq_attn(q, [(kc, vc), (k, v)])
    yc = _gq_attn(qc, [(kc, vc)]) if need_ctx else None
    return (y, wo), (yc, wo)


def _rw_layer(h, hc, mod, modc, p, need_ctx):
    b, _, d = h.shape
    group = lax.broadcasted_iota(jnp.int32, (d, d), 0) // RW_HEAD == lax.broadcasted_iota(jnp.int32, (d, d), 1) // RW_HEAD
    head_of_lane = lax.broadcasted_iota(jnp.int32, (d, LANES), 0) // RW_HEAD
    down = (head_of_lane == lax.broadcasted_iota(jnp.int32, (d, LANES), 1)).astype(MXU_DTYPE)
    p = dict(p, e=group.astype(MXU_DTYPE), head_down=down, head_up=down.T)
    r, v, lw, kd, kn, ba, g = _rw_tok(h, mod, p)
    rc, vc, lwc, kdc, knc, bac, gc = _rw_tok(hc, modc, p)
    zero = jnp.zeros((2, b, d // LANES, LANES, LANES), F32)
    yc0, yc1, state_c = _scan(zero, lwc, knc, bac, kdc, rc, vc)
    y0, y1, _ = _scan(state_c, lw, kn, ba, kd, r, v)
    h = _rw_out(h, mod, y0, y1, kd, g, r, v, p)
    if need_ctx:
        hc = _rw_out(hc, modc, yc0, yc1, kdc, gc, rc, vc, p)
    return h, hc


def kernel(x, c, ctx, c_ctx, ada_w, ada_b, ffn_wg, ffn_wu, ffn_wd, final_g, da_wqkv, da_wo, da_lam, da_subln, gq_wqkv, gq_wo, gq_qk_g, rw_mu, rw_wr, rw_wk, rw_wv, rw_wo, rw_w0, rw_w1, rw_w2, rw_a0, rw_a1, rw_a2, rw_g1, rw_g2, rw_kk, rw_ka, rw_rk, rw_ln_g, rw_ln_b):
    b, _, d = x.shape
    depth = ada_w.shape[0]
    rows = -(-(b + 1) // SUBLANES) * SUBLANES
    cvec = jnp.zeros((rows, d), F32).at[:b].set(c).at[b].set(c_ctx)
    mods = _adaln(cvec, ada_w, ada_b)
    h, hc = x, ctx
    for i in range(depth):
        kind, j = i % 3, i // 3
        need_ctx = i < depth - 1
        mod = mods[i, :b].reshape(b, N_MOD, d)
        modc = mods[i, b:b + 1].reshape(1, N_MOD, d)
        wg, wu, wd = _mx(ffn_wg[i, 0]), _mx(ffn_wu[i, 0]), _mx(ffn_wd[i, 0])
        h = _ffn(h, mod, wg, wu, wd, 0)
        hc = _ffn(hc, modc, wg, wu, wd, 0)
        mixer = mixer_c = None
        if kind == 0:
            lam_init = 0.8 - 0.6 * math.exp(-0.3 * i)
            mixer, mixer_c = _da_layer(h, hc, mod, modc, da_wqkv[j], da_wo[j], da_lam[j], da_subln[j], lam_init, need_ctx)
        elif kind == 1:
            mixer, mixer_c = _gq_layer(h, hc, mod, modc, gq_wqkv[j], gq_wo[j], gq_qk_g[j], need_ctx)
        else:
            p = dict(mu=rw_mu[j], wr=_mx(rw_wr[j]), wk=_mx(rw_wk[j]), wv=_mx(rw_wv[j]), wo=_mx(rw_wo[j]),
                     w0=rw_w0[j], w1=_mx(rw_w1[j]), w2=_mx(rw_w2[j]), a0=rw_a0[j], a1=_mx(rw_a1[j]),
                     a2=_mx(rw_a2[j]), g1=_mx(rw_g1[j]), g2=_mx(rw_g2[j]), kk=rw_kk[j], ka=rw_ka[j],
                     rk=rw_rk[j].reshape(2, d), ln_g=rw_ln_g[j], ln_b=rw_ln_b[j])
            h, hc = _rw_layer(h, hc, mod, modc, p, need_ctx)
        wg, wu, wd = _mx(ffn_wg[i, 1]), _mx(ffn_wu[i, 1]), _mx(ffn_wd[i, 1])
        h = _ffn(h, mod, wg, wu, wd, 6, mixer=mixer, final_gain=final_g if i == depth - 1 else None)
        if need_ctx:
            hc = _ffn(hc, modc, wg, wu, wd, 6, mixer=mixer_c)
    return h
```

```python
import functools
import math

import jax
import jax.numpy as jnp
from jax import lax
from jax.experimental import pallas as pl
from jax.experimental.pallas import tpu as pltpu

F32 = jnp.float32
MXU_DTYPE = jnp.bfloat16

NORM_EPS = 1e-6
ROPE_THETA = 10000.0
GRID_W = 64
N_MOD = 9
DA_HEADS = 8
DA_HEAD_DIM = 64
GQ_HEADS = 8
GQ_KV_HEADS = 2
GQ_HEAD_DIM = 128
GQ_GROUP = GQ_HEADS // GQ_KV_HEADS
RW_HEAD = 64
RW_GN_EPS = 64e-5

LANES = 128
SUBLANES = 8
VMEM_LIMIT_BYTES = 56 * 1024 * 1024

SCAN_CHUNK = 64


def _cparams(n_axes):
    return pltpu.CompilerParams(
        dimension_semantics=("arbitrary",) * n_axes,
        vmem_limit_bytes=VMEM_LIMIT_BYTES,
    )


def _resident(shape):
    zeros = (0,) * len(shape)
    return pl.BlockSpec(shape, lambda *_: zeros)


def _mod_spec(mods, d):
    if mods.shape[0] > 1:
        return pl.BlockSpec((1, N_MOD, d), lambda b, t: (b, 0, 0))
    return pl.BlockSpec((1, N_MOD, d), lambda b, t: (0, 0, 0))


def _rms(x):
    return x * lax.rsqrt(jnp.mean(x * x, axis=-1, keepdims=True) + NORM_EPS)


def _modulate(x, mod_ref, base):
    shift = mod_ref[0, base:base + 1, :]
    scale = mod_ref[0, base + 1:base + 2, :]
    return _rms(x) * (1.0 + scale) + shift


def _dot(a, b):
    return jnp.dot(a, b, preferred_element_type=F32)


def _mx(x):
    return x.astype(MXU_DTYPE)


_NT = (((1,), (1,)), ((), ()))
_NN = (((1,), (0,)), ((), ()))
LOG2E = math.log2(math.e)


def _adaln_kernel(c_ref, w_ref, b_ref, o_ref):
    c = c_ref[...]
    act = _mx(c * jax.nn.sigmoid(c))
    o_ref[0] = _dot(act, _mx(w_ref[0])) + b_ref[0]


def _adaln(cvec, ada_w, ada_b):
    depth, d, n = ada_w.shape
    rows = cvec.shape[0]
    tn = 1024
    return pl.pallas_call(
        _adaln_kernel,
        grid=(depth, n // tn),
        in_specs=[
            pl.BlockSpec((rows, d), lambda i, j: (0, 0)),
            pl.BlockSpec((1, d, tn), lambda i, j: (i, 0, j)),
            pl.BlockSpec((1, 1, tn), lambda i, j: (i, 0, j)),
        ],
        out_specs=pl.BlockSpec((1, rows, tn), lambda i, j: (i, 0, j)),
        out_shape=jax.ShapeDtypeStruct((depth, rows, n), F32),
        compiler_params=_cparams(2),
        name="adaln",
    )(cvec, ada_w, ada_b.reshape(depth, 1, n))


def _ffn_kernel(h_ref, mod_ref, wg_ref, wu_ref, wd_ref, *rest, base, tf, mixer, final):
    rest = list(rest)
    o_ref = rest.pop()
    x = h_ref[0]
    if mixer:
        y_ref, wo_ref = rest[0], rest[1]
        x = x + mod_ref[0, 5:6, :] * _dot(y_ref[0], wo_ref[...])
    xn = _mx(_modulate(x, mod_ref, base))
    f = wg_ref.shape[1]
    acc = jnp.zeros(x.shape, F32)
    for c in range(f // tf):
        g = _dot(xn, wg_ref[:, c * tf:(c + 1) * tf])
        u = _dot(xn, wu_ref[:, c * tf:(c + 1) * tf])
        a = _mx(g * jax.nn.sigmoid(g) * u)
        acc = acc + _dot(a, wd_ref[c * tf:(c + 1) * tf, :])
    gate = mod_ref[0, base + 2:base + 3, :]
    out = x + (0.5 * gate) * acc
    if final:
        out = _rms(out) * rest[-1][...]
    o_ref[0] = out


def _ffn(h, mods, wg, wu, wd, base, mixer=None, final_gain=None, tm=512, tf=256):
    b, t, d = h.shape
    f = wg.shape[1]
    tm = min(tm, t)
    tok = pl.BlockSpec((1, tm, d), lambda i, j: (i, j, 0))
    in_specs = [tok, _mod_spec(mods, d), _resident((d, f)), _resident((d, f)), _resident((f, d))]
    args = [h, mods, wg, wu, wd]
    if mixer is not None:
        in_specs += [tok, _resident(mixer[1].shape)]
        args += list(mixer)
    if final_gain is not None:
        in_specs.append(_resident((1, d)))
        args.append(final_gain.reshape(1, d))
    return pl.pallas_call(
        functools.partial(_ffn_kernel, base=base, tf=tf, mixer=mixer is not None, final=final_gain is not None),
        grid=(b, t // tm),
        in_specs=in_specs,
        out_specs=tok,
        out_shape=jax.ShapeDtypeStruct((b, t, d), F32),
        compiler_params=_cparams(2),
        name="ffn",
    )(*args)


def _rope_tables(n_tokens, head_dim):
    rows = n_tokens // GRID_W
    row = jnp.repeat(jnp.arange(rows, dtype=jnp.int32), GRID_W).astype(F32)
    col = jnp.tile(jnp.arange(GRID_W, dtype=jnp.int32), rows).astype(F32)
    n_freq = head_dim // 4
    inv = jnp.power(ROPE_THETA, -jnp.arange(n_freq, dtype=F32) / n_freq)
    ang = jnp.concatenate([row[:, None] * inv, col[:, None] * inv], axis=-1)
    cos, sin = jnp.cos(ang), jnp.sin(ang)
    reps = LANES // head_dim
    cos_t = jnp.tile(jnp.concatenate([cos, cos], axis=-1), (1, reps))
    sin_t = jnp.tile(jnp.concatenate([-sin, sin], axis=-1), (1, reps))
    return cos_t, sin_t


def _rope_slab(x, cos, sin, half):
    if 2 * half == LANES:
        swapped = pltpu.roll(x, half, 1)
    else:
        lane = lax.broadcasted_iota(jnp.int32, x.shape, 1)
        first = (lane & (2 * half - 1)) < half
        swapped = jnp.where(first, pltpu.roll(x, LANES - half, 1), pltpu.roll(x, half, 1))
    return x * cos + swapped * sin


def _da_proj_kernel(h_ref, mod_ref, w_ref, *rest, rope):
    if rope:
        cos_ref, sin_ref, q_ref, k_ref, v_ref = rest
        cos, sin = cos_ref[...], sin_ref[...]
    else:
        q_ref, k_ref, v_ref = rest
    d = h_ref.shape[2]
    u = _mx(_modulate(h_ref[0], mod_ref, 3))
    scale = DA_HEAD_DIM ** -0.5 * LOG2E
    for idx, (o_ref, rot, mul) in enumerate(((q_ref, True, scale), (k_ref, True, 1.0), (v_ref, False, 1.0))):
        y = _dot(u, w_ref[:, idx * d:(idx + 1) * d])
        for j in range(d // LANES):
            ys = y[:, j * LANES:(j + 1) * LANES]
            if rot and rope:
                ys = _rope_slab(ys, cos, sin, DA_HEAD_DIM // 2)
            if mul != 1.0:
                ys = ys * mul
            o_ref[0, :, j * LANES:(j + 1) * LANES] = ys.astype(o_ref.dtype)


def _da_proj(h, mods, wqkv, tables, tm=512):
    b, t, d = h.shape
    tm = min(tm, t)
    rope = tables is not None
    tok = pl.BlockSpec((1, tm, d), lambda i, j: (i, j, 0))
    in_specs = [tok, _mod_spec(mods, d), _resident(wqkv.shape)]
    args = [h, mods, wqkv]
    if rope:
        tab = pl.BlockSpec((tm, LANES), lambda i, j: (j, 0))
        in_specs += [tab, tab]
        args += list(tables)
    out = jax.ShapeDtypeStruct((b, t, d), MXU_DTYPE)
    return pl.pallas_call(
        functools.partial(_da_proj_kernel, rope=rope),
        grid=(b, t // tm),
        in_specs=in_specs,
        out_specs=[tok, tok, tok],
        out_shape=[out, out, out],
        compiler_params=_cparams(2),
        name="da_proj",
    )(*args)


def _gq_proj_kernel(h_ref, mod_ref, w_ref, g_ref, *rest, rope):
    if rope:
        cos_ref, sin_ref, q_ref, k_ref, v_ref = rest
        cos, sin = cos_ref[...], sin_ref[...]
    else:
        q_ref, k_ref, v_ref = rest
    u = _mx(_modulate(h_ref[0], mod_ref, 3))
    nq = GQ_HEADS * GQ_HEAD_DIM
    nk = GQ_KV_HEADS * GQ_HEAD_DIM
    scale = GQ_HEAD_DIM ** -0.5 * LOG2E
    plan = ((q_ref, 0, nq, 0, scale), (k_ref, nq, nk, 1, 1.0), (v_ref, nq + nk, nk, None, 1.0))
    for o_ref, start, width, gain_row, mul in plan:
        y = _dot(u, w_ref[:, start:start + width])
        for j in range(width // LANES):
            ys = y[:, j * LANES:(j + 1) * LANES]
            if gain_row is not None:
                ys = _rms(ys) * g_ref[gain_row:gain_row + 1, :]
                if rope:
                    ys = _rope_slab(ys, cos, sin, GQ_HEAD_DIM // 2)
            if mul != 1.0:
                ys = ys * mul
            o_ref[0, :, j * LANES:(j + 1) * LANES] = ys.astype(o_ref.dtype)


def _gq_proj(h, mods, wqkv, qk_g, tables, tm=512):
    b, t, d = h.shape
    tm = min(tm, t)
    rope = tables is not None
    nq = GQ_HEADS * GQ_HEAD_DIM
    nk = GQ_KV_HEADS * GQ_HEAD_DIM
    tok = pl.BlockSpec((1, tm, d), lambda i, j: (i, j, 0))
    in_specs = [tok, _mod_spec(mods, d), _resident(wqkv.shape), _resident(qk_g.shape)]
    args = [h, mods, wqkv, qk_g]
    if rope:
        tab = pl.BlockSpec((tm, LANES), lambda i, j: (j, 0))
        in_specs += [tab, tab]
        args += list(tables)
    return pl.pallas_call(
        functools.partial(_gq_proj_kernel, rope=rope),
        grid=(b, t // tm),
        in_specs=in_specs,
        out_specs=[pl.BlockSpec((1, tm, nq), lambda i, j: (i, j, 0)),
                   pl.BlockSpec((1, tm, nk), lambda i, j: (i, j, 0)),
                   pl.BlockSpec((1, tm, nk), lambda i, j: (i, j, 0))],
        out_shape=[jax.ShapeDtypeStruct((b, t, nq), MXU_DTYPE),
                   jax.ShapeDtypeStruct((b, t, nk), MXU_DTYPE),
                   jax.ShapeDtypeStruct((b, t, nk), MXU_DTYPE)],
        compiler_params=_cparams(2),
        name="gq_proj",
    )(*args)


def _attn_pipeline(n_blocks, k_refs, s_scr, make_qs, finish):
    def scores(i):
        qs = make_qs(i)
        off = 0
        for k_ref in k_refs:
            n = k_ref.shape[1]
            s_scr[i % 2, :, off:off + n] = lax.dot_general(qs, k_ref[0], _NT, preferred_element_type=F32)
            off += n

    scores(0)
    for i in range(n_blocks):
        if i + 1 < n_blocks:
            scores(i + 1)
        s = s_scr[i % 2]
        p = jnp.exp2(s - jnp.max(s, axis=-1, keepdims=True))
        finish(i, p, jnp.sum(p, axis=-1, keepdims=True))


def _pv(a, v_refs):
    out, off = None, 0
    for v_ref in v_refs:
        n = v_ref.shape[1]
        part = _dot(a[:, off:off + n], v_ref[0])
        out = part if out is None else out + part
        off += n
    return out


def _da_attn_kernel(lam_ref, g_ref, q_ref, *rest, nseg, rb, lam_init):
    k_refs = [rest[2 * i] for i in range(nseg)]
    v_refs = [rest[2 * i + 1] for i in range(nseg)]
    o_ref, s_scr = rest[2 * nseg:]
    lv = lam_ref[...]
    lam = (jnp.exp(jnp.sum(lv[0:1] * lv[1:2], keepdims=True))
           - jnp.exp(jnp.sum(lv[2:3] * lv[3:4], keepdims=True)) + lam_init)
    gain = g_ref[...] * (1.0 - lam_init)
    lane = lax.broadcasted_iota(jnp.int32, (rb, LANES), 1)

    def make_qs(i):
        q = q_ref[0, i * rb:(i + 1) * rb, :]
        zero = jnp.zeros_like(q)
        return jnp.concatenate([jnp.where(lane < DA_HEAD_DIM, q, zero), jnp.where(lane >= DA_HEAD_DIM, q, zero)], axis=0)

    def finish(i, p, l):
        o = _pv(_mx(p), v_refs) / l
        y = _rms(o[:rb] - lam * o[rb:]) * gain
        o_ref[0, i * rb:(i + 1) * rb, :] = y.astype(o_ref.dtype)

    _attn_pipeline(q_ref.shape[1] // rb, k_refs, s_scr, make_qs, finish)


def _da_attn(q, kv_segs, lam_vecs, subln, lam_init, tq=1024, rb=128):
    b, t, d = q.shape
    tq = min(tq, t)
    rb = min(rb, tq)
    in_specs = [_resident(lam_vecs.shape), _resident(subln.shape),
                pl.BlockSpec((1, tq, LANES), lambda i, h, j: (i, j, h))]
    args = [lam_vecs, subln, q]
    n_keys = 0
    for k, v in kv_segs:
        n = k.shape[1]
        n_keys += n
        seg = pl.BlockSpec((1, n, LANES), lambda i, h, j: (i, 0, h))
        in_specs += [seg, seg]
        args += [k, v]
    return pl.pallas_call(
        functools.partial(_da_attn_kernel, nseg=len(kv_segs), rb=rb, lam_init=lam_init),
        grid=(b, DA_HEADS, t // tq),
        in_specs=in_specs,
        out_specs=pl.BlockSpec((1, tq, LANES), lambda i, h, j: (i, j, h)),
        out_shape=jax.ShapeDtypeStruct((b, t, d), MXU_DTYPE),
        scratch_shapes=[pltpu.VMEM((2, 2 * rb, n_keys), F32)],
        compiler_params=_cparams(3),
        name="da_attn",
    )(*args)


def _gq_attn_kernel(q_ref, *rest, nseg, rb):
    k_refs = [rest[2 * i] for i in range(nseg)]
    v_refs = [rest[2 * i + 1] for i in range(nseg)]
    o_ref, s_scr = rest[2 * nseg:]

    def make_qs(i):
        return jnp.concatenate([q_ref[0, i * rb:(i + 1) * rb, m * LANES:(m + 1) * LANES] for m in range(GQ_GROUP)], axis=0)

    def finish(i, p, l):
        o = _pv(_mx(p), v_refs) / l
        for m in range(GQ_GROUP):
            o_ref[0, i * rb:(i + 1) * rb, m * LANES:(m + 1) * LANES] = o[m * rb:(m + 1) * rb].astype(o_ref.dtype)

    _attn_pipeline(q_ref.shape[1] // rb, k_refs, s_scr, make_qs, finish)


def _gq_attn(q, kv_segs, tq=512, rb=32):
    b, t, d = q.shape
    tq = min(tq, t)
    rb = min(rb, tq)
    gw = GQ_GROUP * GQ_HEAD_DIM
    in_specs = [pl.BlockSpec((1, tq, gw), lambda i, g, j: (i, j, g))]
    args = [q]
    n_keys = 0
    for k, v in kv_segs:
        n = k.shape[1]
        n_keys += n
        seg = pl.BlockSpec((1, n, LANES), lambda i, g, j: (i, 0, g))
        in_specs += [seg, seg]
        args += [k, v]
    return pl.pallas_call(
        functools.partial(_gq_attn_kernel, nseg=len(kv_segs), rb=rb),
        grid=(b, GQ_KV_HEADS, t // tq),
        in_specs=in_specs,
        out_specs=pl.BlockSpec((1, tq, gw), lambda i, g, j: (i, j, g)),
        out_shape=jax.ShapeDtypeStruct((b, t, d), MXU_DTYPE),
        scratch_shapes=[pltpu.VMEM((2, GQ_GROUP * rb, n_keys), F32)],
        compiler_params=_cparams(3),
        name="gq_attn",
    )(*args)


def _head_sum(x, e_ref):
    return _dot(_mx(x), e_ref[...])


def _rw_tok_kernel(h_ref, hp_ref, hn_ref, mod_ref, mu_ref, wr_ref, wk_ref, wv_ref,
                   w0_ref, w1_ref, w2_ref, a0_ref, a1_ref, a2_ref, g1_ref, g2_ref, kk_ref, ka_ref, e_ref,
                   r_out, v_out, lw_out, kd_out, kn_out, ba_out, g_out):
    t = pl.program_id(1)
    nt = pl.num_programs(1)
    tm = h_ref.shape[1]
    z = _modulate(h_ref[0], mod_ref, 3)
    zp = _modulate(hp_ref[0], mod_ref, 3)[SUBLANES - 1:SUBLANES]
    zn = _modulate(hn_ref[0], mod_ref, 3)[0:1]
    zp = jnp.where(t > 0, zp, jnp.zeros_like(zp))
    zn = jnp.where(t < nt - 1, zn, jnp.zeros_like(zn))
    row = lax.broadcasted_iota(jnp.int32, z.shape, 0)
    dp = jnp.where(row == 0, zp, pltpu.roll(z, 1, 0)) - z
    dn = jnp.where(row == tm - 1, zn, pltpu.roll(z, tm - 1, 0)) - z

    def mix(i):
        return z + dp * mu_ref[0, i:i + 1, :] + dn * mu_ref[1, i:i + 1, :]

    r = _dot(_mx(mix(0)), wr_ref[...])
    k = _dot(_mx(mix(2)), wk_ref[...])
    v = _dot(_mx(mix(3)), wv_ref[...])
    xw, xa, xg = _mx(mix(1)), _mx(mix(4)), _mx(mix(5))
    r_out[0] = r
    v_out[0] = v
    for d in range(2):
        x = -(w0_ref[d:d + 1, :] + _dot(_mx(jnp.tanh(_dot(xw, w1_ref[d]))), w2_ref[d]))
        softplus = jnp.maximum(x, 0.0) + jnp.log(1.0 + jnp.exp(-jnp.abs(x)))
        lw_out[d, 0] = -jnp.exp(-softplus - 0.5)
        a = jax.nn.sigmoid(a0_ref[d:d + 1, :] + _dot(_mx(_dot(xa, a1_ref[d])), a2_ref[d]))
        g_out[d, 0] = _dot(_mx(jax.nn.sigmoid(_dot(xg, g1_ref[d]))), g2_ref[d])
        kx = k * kk_ref[d:d + 1, :]
        kn = kx / jnp.maximum(jnp.sqrt(_head_sum(kx * kx, e_ref)), 1e-12)
        kn_out[d, 0] = kn
        ba_out[d, 0] = kn * a
        kd_out[d, 0] = k * (1.0 + (a - 1.0) * ka_ref[d:d + 1, :])


def _rw_tok(h, mods, p, tm=256):
    b, t, d = h.shape
    tm = min(tm, t)
    nb8 = t // SUBLANES
    per8 = tm // SUBLANES
    tok = pl.BlockSpec((1, tm, d), lambda i, j: (i, j, 0))
    prev = pl.BlockSpec((1, SUBLANES, d), lambda i, j: (i, jnp.maximum(j * per8 - 1, 0), 0))
    nxt = pl.BlockSpec((1, SUBLANES, d), lambda i, j: (i, jnp.minimum((j + 1) * per8, nb8 - 1), 0))
    names = ("mu", "wr", "wk", "wv", "w0", "w1", "w2", "a0", "a1", "a2", "g1", "g2", "kk", "ka", "e")
    weights = [p[n] for n in names]
    both = pl.BlockSpec((2, 1, tm, d), lambda i, j: (0, i, j, 0))
    one = jax.ShapeDtypeStruct((b, t, d), F32)
    two = jax.ShapeDtypeStruct((2, b, t, d), F32)
    return pl.pallas_call(
        _rw_tok_kernel,
        grid=(b, t // tm),
        in_specs=[tok, prev, nxt, _mod_spec(mods, d)] + [_resident(w.shape) for w in weights],
        out_specs=[tok, tok, both, both, both, both, both],
        out_shape=[one, one, two, two, two, two, two],
        compiler_params=_cparams(2),
        name="rw_tok",
    )(h, h, h, mods, *weights)


def _mm(a, b, dims=_NN):
    return lax.dot_general(_mx(a), _mx(b), dims, preferred_element_type=F32)


def _split(x):
    hi = _mx(x)
    return hi, _mx(x - hi.astype(F32))


def _cumsum_mm(tri, x):
    hi, lo = _split(x)
    return _dot(tri, hi) + _dot(tri, lo)


def _scan_masks(rev):
    L = SCAN_CHUNK
    r1 = lax.broadcasted_iota(jnp.int32, (L, LANES), 0)
    c1 = lax.broadcasted_iota(jnp.int32, (L, LANES), 1) & (L - 1)
    tr = lax.broadcasted_iota(jnp.int32, (L, L), 0)
    tc = lax.broadcasted_iota(jnp.int32, (L, L), 1)
    rt6 = lax.broadcasted_iota(jnp.int32, (2 * L, LANES), 0) & (L - 1)
    ct6 = lax.broadcasted_iota(jnp.int32, (2 * L, LANES), 1) & (L - 1)
    late, early = (ct6, rt6) if rev else (rt6, ct6)
    off = [((rt6 >> (i + 1)) == (ct6 >> (i + 1))) & ((late & (1 << i)) != 0) & ((early & (1 << i)) == 0)
           for i in range(int(math.log2(L)))]
    return dict(
        incl=(c1 >= r1) if rev else (c1 <= r1),
        strict=(c1 > r1) if rev else (c1 < r1),
        tri=jnp.where((tc >= tr) if rev else (tc <= tr), 1.0, 0.0).astype(MXU_DTYPE),
        off=off,
    )


def _scan_kernel(s0_ref, *refs):
    L = SCAN_CHUNK
    in_refs = (refs[0:6], refs[6:12])
    y_refs = refs[12:14]
    sf_ref, s_scr = refs[14:]
    n_pairs = s_scr.shape[1]
    n_chunks = in_refs[0][0].shape[2] // L

    @pl.when(pl.program_id(1) == 0)
    def _():
        s_scr[...] = s0_ref[:, 0]

    masks = (_scan_masks(False), _scan_masks(True))
    r2 = lax.broadcasted_iota(jnp.int32, (2 * L, LANES), 0)
    c2 = lax.broadcasted_iota(jnp.int32, (2 * L, LANES), 1)
    own = (r2 < L) == (c2 < L)
    eye = jnp.where(r2 == c2, 1.0, 0.0).astype(F32)
    chains = [(d, p) for d in range(2) for p in range(n_pairs)]
    mk = [masks[d] for d, _ in chains]

    def bd(x):
        return jnp.where(own, jnp.concatenate([x, x], axis=0), 0.0)

    def each(fn, *lists):
        return [fn(*args) for args in zip(*lists)]

    def chunk(ci, carry):
        rows = (pl.ds(pl.multiple_of(ci * L, L), L), pl.ds(pl.multiple_of((n_chunks - 1 - ci) * L, L), L))

        def load(which, lead):
            out = []
            for d, p in chains:
                ref = in_refs[d][which]
                idx = (0, 0) if lead else (0,)
                out.append(ref[idx + (rows[d], slice(p * LANES, (p + 1) * LANES))])
            return out

        lw = load(0, True)
        cin = each(lambda m, x: _cumsum_mm(m["tri"], x), mk, lw)
        e_in = each(jnp.exp, cin)
        e_neg = each(lambda c: jnp.exp(-c), cin)
        at = each(lambda k_, c, x: -k_ * jnp.exp(c - x), load(1, True), cin, lw)
        rt = each(lambda r_, e: r_ * e, load(4, False), e_in)
        bt = each(lambda b_, e: b_ * e, load(2, True), e_neg)
        kt = each(lambda k_, e: k_ * e, load(3, True), e_neg)
        wl = [e[0:1] if d else e[L - 1:L] for (d, _), e in zip(chains, e_in)]
        g = each(lambda a, r_, b_, k_: _mm(jnp.concatenate([a, r_], axis=0),
                                           jnp.concatenate([bd(b_), bd(k_)], axis=0), _NT), at, rt, bt, kt)
        n_bd = each(lambda m, x: bd(jnp.where(m["strict"], x[:L, :LANES], 0.0)), mk, g)
        arb = each(lambda m, x: bd(jnp.where(m["incl"], x[L:, :LANES], 0.0)), mk, g)
        aak = each(lambda m, x: bd(jnp.where(m["strict"], x[:L, LANES:], 0.0)), mk, g)
        ark = each(lambda m, x: bd(jnp.where(m["incl"], x[L:, LANES:], 0.0)), mk, g)
        tinv = each(lambda m, n: eye + jnp.where(m["off"][0], n, 0.0), mk, n_bd)
        for lvl in range(1, int(math.log2(L))):
            inner = each(lambda m, n, t_: _mm(jnp.where(m["off"][lvl], n, 0.0), t_), mk, n_bd, tinv)
            tinv = each(lambda t_, x: t_ + _mm(t_, x), tinv, inner)
        v_sd = each(bd, load(5, False))
        akv = each(_mm, aak, v_sd)
        pq = each(lambda t_, a, x: _mm(t_, jnp.concatenate([bd(a), x], axis=1)), tinv, at, akv)
        bh = each(lambda b_, w: bd(b_ * w), bt, wl)
        kh = each(lambda k_, w: bd(k_ * w), kt, wl)
        qv = each(lambda x, v_: jnp.concatenate([x[:, LANES:], v_], axis=0), pq, v_sd)
        ptb = each(lambda x, b_: _mm(x[:, :LANES].T, b_), pq, bh)
        c_mat = each(lambda x, b_, k_: _mm(x.T, jnp.concatenate([b_, k_], axis=0)), qv, bh, kh)
        ry = each(lambda r_, a, x: bd(r_) + _mm(a, x[:, :LANES]), rt, arb, pq)
        y0 = each(lambda a, k_, x: _mm(jnp.concatenate([a, k_], axis=1), x), arb, ark, qv)
        s_in = [s_scr[d, p] for d, p in chains]
        ysd = each(lambda r_, s_, y_: _mm(r_, s_, _NT) + y_, ry, s_in, y0)
        for (d, p), y_ in zip(chains, ysd):
            y_refs[d][0, rows[d], p * LANES:(p + 1) * LANES] = y_[:L] + y_[L:]
        s_out = each(lambda s_, w, x, c: s_ * w + _mm(s_, x) + c, s_in, wl, ptb, c_mat)
        for (d, p), s_ in zip(chains, s_out):
            s_scr[d, p] = s_
        return carry

    lax.fori_loop(0, n_chunks, chunk, 0)
    sf_ref[:, 0] = s_scr[...]


def _scan(s0, lw, kn, ba, kd, r, v, tb=256):
    b, t, d = r.shape
    tb = min(tb, t)
    nb = t // tb
    n_pairs = d // LANES
    state = pl.BlockSpec((2, 1, n_pairs, LANES, LANES), lambda i, j: (0, i, 0, 0, 0))
    in_specs = [state]
    args = [s0]
    out_specs = []
    for direction in range(2):
        tmap = (lambda j: nb - 1 - j) if direction else (lambda j: j)
        dirspec = pl.BlockSpec((1, 1, tb, d), lambda i, j, direction=direction, tmap=tmap: (direction, i, tmap(j), 0))
        tok = pl.BlockSpec((1, tb, d), lambda i, j, tmap=tmap: (i, tmap(j), 0))
        in_specs += [dirspec, dirspec, dirspec, dirspec, tok, tok]
        args += [lw, kn, ba, kd, r, v]
        out_specs.append(tok)
    out_specs.append(state)
    y = jax.ShapeDtypeStruct((b, t, d), F32)
    return pl.pallas_call(
        _scan_kernel,
        grid=(b, nb),
        in_specs=in_specs,
        out_specs=out_specs,
        out_shape=[y, y, jax.ShapeDtypeStruct((2, b, n_pairs, LANES, LANES), F32)],
        scratch_shapes=[pltpu.VMEM((2, n_pairs, LANES, LANES), F32)],
        compiler_params=_cparams(2),
        name="rw_scan",
    )(*args)


def _head_sum_2step(x, hd_ref, hu_ref):
    hi, lo = _split(_dot(_mx(x), hd_ref[...]))
    return _dot(hi, hu_ref[...]) + _dot(lo, hu_ref[...])


def _rw_out_kernel(h_ref, mod_ref, y0_ref, y1_ref, kd_ref, g_ref, r_ref, v_ref,
                   rk_ref, lng_ref, lnb_ref, hd_ref, hu_ref, wo_ref, o_ref):
    r = r_ref[0]
    v = v_ref[0]
    inv_n = 1.0 / RW_HEAD
    total = jnp.zeros(r.shape, F32)
    for d, y_ref in enumerate((y0_ref, y1_ref)):
        y = y_ref[0]
        yc = y - _head_sum_2step(y, hd_ref, hu_ref) * inv_n
        var = _head_sum_2step(yc * yc, hd_ref, hu_ref) * inv_n
        yn = yc * lax.rsqrt(var + RW_GN_EPS)
        bonus = _head_sum_2step(r * kd_ref[d, 0] * rk_ref[d:d + 1, :], hd_ref, hu_ref) * v
        out = yn * lng_ref[d:d + 1, :] + lnb_ref[d:d + 1, :] + bonus
        total = total + out * g_ref[d, 0]
    gate = mod_ref[0, 5:6, :]
    o_ref[0] = h_ref[0] + gate * _dot(_mx(total), wo_ref[...])


def _rw_out(h, mods, y0, y1, kd, g, r, v, p, tm=256):
    b, t, d = h.shape
    tm = min(tm, t)
    tok = pl.BlockSpec((1, tm, d), lambda i, j: (i, j, 0))
    both = pl.BlockSpec((2, 1, tm, d), lambda i, j: (0, i, j, 0))
    weights = [p["rk"], p["ln_g"], p["ln_b"], p["head_down"], p["head_up"], p["wo"]]
    return pl.pallas_call(
        _rw_out_kernel,
        grid=(b, t // tm),
        in_specs=[tok, _mod_spec(mods, d), tok, tok, both, both, tok, tok] + [_resident(w.shape) for w in weights],
        out_specs=tok,
        out_shape=jax.ShapeDtypeStruct((b, t, d), F32),
        compiler_params=_cparams(2),
        name="rw_out",
    )(h, mods, y0, y1, kd, g, r, v, *weights)


def _da_layer(h, hc, mod, modc, wqkv, wo, lam_vecs, subln, lam_init, need_ctx):
    s = h.shape[1]
    wqkv, wo = _mx(wqkv), _mx(wo)
    subln = subln.reshape(1, -1)
    q, k, v = _da_proj(h, mod, wqkv, _rope_tables(s, DA_HEAD_DIM))
    qc, kc, vc = _da_proj(hc, modc, wqkv, None)
    y = _da_attn(q, [(kc, vc), (k, v)], lam_vecs, subln, lam_init)
    yc = _da_attn(qc, [(kc, vc)], lam_vecs, subln, lam_init) if need_ctx else None
    return (y, wo), (yc, wo)


def _gq_layer(h, hc, mod, modc, wqkv, wo, qk_g, need_ctx):
    s = h.shape[1]
    wqkv, wo = _mx(wqkv), _mx(wo)
    q, k, v = _gq_proj(h, mod, wqkv, qk_g, _rope_tables(s, GQ_HEAD_DIM))
    qc, kc, vc = _gq_proj(hc, modc, wqkv, qk_g, None)
    y = _gq_attn(q, [(kc, vc), (k, v)])
    yc = _gq_attn(qc, [(kc, vc)]) if need_ctx else None
    return (y, wo), (yc, wo)


def _rw_layer(h, hc, mod, modc, p, need_ctx):
    b, _, d = h.shape
    group = lax.broadcasted_iota(jnp.int32, (d, d), 0) // RW_HEAD == lax.broadcasted_iota(jnp.int32, (d, d), 1) // RW_HEAD
    head_of_lane = lax.broadcasted_iota(jnp.int32, (d, LANES), 0) // RW_HEAD
    down = (head_of_lane == lax.broadcasted_iota(jnp.int32, (d, LANES), 1)).astype(MXU_DTYPE)
    p = dict(p, e=group.astype(MXU_DTYPE), head_down=down, head_up=down.T)
    r, v, lw, kd, kn, ba, g = _rw_tok(h, mod, p)
    rc, vc, lwc, kdc, knc, bac, gc = _rw_tok(hc, modc, p)
    zero = jnp.zeros((2, b, d // LANES, LANES, LANES), F32)
    yc0, yc1, state_c = _scan(zero, lwc, knc, bac, kdc, rc, vc)
    y0, y1, _ = _scan(state_c, lw, kn, ba, kd, r, v)
    h = _rw_out(h, mod, y0, y1, kd, g, r, v, p)
    if need_ctx:
        hc = _rw_out(hc, modc, yc0, yc1, kdc, gc, rc, vc, p)
    return h, hc


def kernel(x, c, ctx, c_ctx, ada_w, ada_b, ffn_wg, ffn_wu, ffn_wd, final_g, da_wqkv, da_wo, da_lam, da_subln, gq_wqkv, gq_wo, gq_qk_g, rw_mu, rw_wr, rw_wk, rw_wv, rw_wo, rw_w0, rw_w1, rw_w2, rw_a0, rw_a1, rw_a2, rw_g1, rw_g2, rw_kk, rw_ka, rw_rk, rw_ln_g, rw_ln_b):
    b, _, d = x.shape
    depth = ada_w.shape[0]
    rows = -(-(b + 1) // SUBLANES) * SUBLANES
    cvec = jnp.zeros((rows, d), F32).at[:b].set(c).at[b].set(c_ctx)
    mods = _adaln(cvec, ada_w, ada_b)
    h, hc = x, ctx
    for i in range(depth):
        kind, j = i % 3, i // 3
        need_ctx = i < depth - 1
        mod = mods[i, :b].reshape(b, N_MOD, d)
        modc = mods[i, b:b + 1].reshape(1, N_MOD, d)
        wg, wu, wd = _mx(ffn_wg[i, 0]), _mx(ffn_wu[i, 0]), _mx(ffn_wd[i, 0])
        h = _ffn(h, mod, wg, wu, wd, 0)
        hc = _ffn(hc, modc, wg, wu, wd, 0)
        mixer = mixer_c = None
        if kind == 0:
            lam_init = 0.8 - 0.6 * math.exp(-0.3 * i)
            mixer, mixer_c = _da_layer(h, hc, mod, modc, da_wqkv[j], da_wo[j], da_lam[j], da_subln[j], lam_init, need_ctx)
        elif kind == 1:
            mixer, mixer_c = _gq_layer(h, hc, mod, modc, gq_wqkv[j], gq_wo[j], gq_qk_g[j], need_ctx)
        else:
            p = dict(mu=rw_mu[j], wr=_mx(rw_wr[j]), wk=_mx(rw_wk[j]), wv=_mx(rw_wv[j]), wo=_mx(rw_wo[j]),
                     w0=rw_w0[j], w1=_mx(rw_w1[j]), w2=_mx(rw_w2[j]), a0=rw_a0[j], a1=_mx(rw_a1[j]),
                     a2=_mx(rw_a2[j]), g1=_mx(rw_g1[j]), g2=_mx(rw_g2[j]), kk=rw_kk[j], ka=rw_ka[j],
                     rk=rw_rk[j].reshape(2, d), ln_g=rw_ln_g[j], ln_b=rw_ln_b[j])
            h, hc = _rw_layer(h, hc, mod, modc, p, need_ctx)
        wg, wu, wd = _mx(ffn_wg[i, 1]), _mx(ffn_wu[i, 1]), _mx(ffn_wd[i, 1])
        h = _ffn(h, mod, wg, wu, wd, 6, mixer=mixer, final_gain=final_g if i == depth - 1 else None)
        if need_ctx:
            hc = _ffn(hc, modc, wg, wu, wd, 6, mixer=mixer_c)
    return h
```
